```python
import math
import jax, jax.numpy as jnp
from jax import lax
import numpy as np

D_MODEL = 1024
BATCH = 2
SEQ = 8192
DEPTH = 1

ATTN_HEADS = 8
ATTN_KV_HEADS = 2
ATTN_HEAD_DIM = 128
IDX_HEADS = 4
IDX_HEAD_DIM = 64
TOPK_MAX = 256
Q_BLOCK = 128
ROPE_THETA = 500000.0
ROPE_FRACTION = 4
SSD_EXPAND = 2
SSD_D_INNER = SSD_EXPAND * D_MODEL
SSD_HEAD_DIM = 64
SSD_HEADS = SSD_D_INNER // SSD_HEAD_DIM
SSD_GROUPS = 4
SSD_STATE = 128
SSD_CONV = 4
SSD_CHUNK = 128
SSD_CONV_CH = SSD_D_INNER + 2 * SSD_GROUPS * SSD_STATE
MEM_LEN = 256
MEM_HEADS = 4
MEM_HEAD_DIM = D_MODEL // MEM_HEADS
PEER_HEADS = 8
PEER_N_KEYS = 128
PEER_N_EXPERTS = PEER_N_KEYS * PEER_N_KEYS
PEER_KEY_DIM = 128
PEER_TOPK = 16
PEER_TOKEN_BLOCK = 128
N_BRANCHES = 2
EPS = 1e-6

IN_SPLIT_SIZES = (
    ATTN_HEADS * ATTN_HEAD_DIM,
    ATTN_KV_HEADS * ATTN_HEAD_DIM,
    ATTN_KV_HEADS * ATTN_HEAD_DIM,
    IDX_HEADS * IDX_HEAD_DIM,
    IDX_HEAD_DIM,
    IDX_HEADS,
    SSD_D_INNER,
    SSD_CONV_CH,
    SSD_HEADS,
    N_BRANCHES * D_MODEL,
)
IN_WIDTH = sum(IN_SPLIT_SIZES)

kernel_name = "hybrid_dsa_ssd_peer_block"


def _in_offsets():
    offs, acc = [], 0
    for s in IN_SPLIT_SIZES[:-1]:
        acc += s
        offs.append(acc)
    return offs


def rmsnorm(x, g):
    xf = x.astype(jnp.float32)
    y = xf * lax.rsqrt(jnp.mean(xf * xf, axis=-1, keepdims=True) + EPS)
    return (y * g.astype(jnp.float32)).astype(x.dtype)


def partial_rope(x, pos):
    d = x.shape[-1]
    rot = d // ROPE_FRACTION
    half = rot // 2
    inv = ROPE_THETA ** (-2.0 * jnp.arange(half, dtype=jnp.float32) / rot)
    ang = pos.astype(jnp.float32)[..., None] * inv
    cos = jnp.cos(ang)[:, :, None, :]
    sin = jnp.sin(ang)[:, :, None, :]
    xf = x.astype(jnp.float32)
    x1 = xf[..., :half]
    x2 = xf[..., half:rot]
    out = jnp.concatenate([x1 * cos - x2 * sin, x2 * cos + x1 * sin, xf[..., rot:]], axis=-1)
    return out.astype(x.dtype)


def dsa_attention(q, k, v, iq, ik, iw):
    b, t, h, dh = q.shape
    kv = k.shape[2]
    grp = h // kv
    n_sel = min(TOPK_MAX, t // 4)
    nblk = t // Q_BLOCK
    idx_scale = IDX_HEAD_DIM ** -0.5
    w_scale = IDX_HEADS ** -0.5
    att_scale = dh ** -0.5
    key_idx = jnp.arange(t)

    def to_blocks(a):
        return a.reshape(b, nblk, Q_BLOCK, *a.shape[2:]).swapaxes(0, 1)

    def block(args):
        qb, iqb, iwb, start = args
        qpos = start + jnp.arange(Q_BLOCK)
        s = jax.nn.relu(jnp.einsum('bqhd,bsd->bqhs', iqb, ik).astype(jnp.float32) * idx_scale)
        score = jnp.einsum('bqhs,bqh->bqs', s, iwb.astype(jnp.float32) * w_scale)
        causal = key_idx[None, :] <= qpos[:, None]
        score = jnp.where(causal[None], score, -jnp.inf)
        _, sel = lax.top_k(score, n_sel)
        valid = sel <= qpos[None, :, None]
        ks = jax.vmap(lambda a, i: a[i])(k, sel)
        vs = jax.vmap(lambda a, i: a[i])(v, sel)
        qg = qb.reshape(b, Q_BLOCK, kv, grp, dh)
        logits = jnp.einsum('bqcgd,bqncd->bqcgn', qg, ks).astype(jnp.float32) * att_scale
        logits = jnp.where(valid[:, :, None, None, :], logits, -jnp.inf)
        p = jax.nn.softmax(logits, axis=-1).astype(vs.dtype)
        o = jnp.einsum('bqcgn,bqncd->bqcgd', p, vs)
        return o.reshape(b, Q_BLOCK, h * dh)

    starts = jnp.arange(nblk) * Q_BLOCK
    out = lax.map(block, (to_blocks(q), to_blocks(iq), to_blocks(iw), starts))
    return out.swapaxes(0, 1).reshape(b, t, h * dh)


def ssd_chunked(xs, dt, a, bm, cm):
    b, t, h, p = xs.shape
    g, n = bm.shape[2], bm.shape[3]
    j = h // g
    L = SSD_CHUNK
    nc = t // L
    xdt = (xs * dt[..., None]).reshape(b, nc, L, g, j, p)
    adt = (dt * a).reshape(b, nc, L, g, j)
    bc = bm.reshape(b, nc, L, g, n)
    cc = cm.reshape(b, nc, L, g, n)
    acs = jnp.cumsum(adt, axis=2)
    tri = jnp.tril(jnp.ones((L, L), dtype=bool))
    seg = acs[:, :, :, None] - acs[:, :, None, :]
    decay = jnp.exp(jnp.where(tri[None, None, :, :, None, None], seg, -jnp.inf))
    cb = jnp.einsum('bclgn,bcsgn->bclsg', cc, bc)
    y_diag = jnp.einsum('bclsgj,bcsgjp->bclgjp', cb[..., None] * decay, xdt)
    decay_to_end = jnp.exp(acs[:, :, -1:] - acs)
    states = jnp.einsum('bcsgn,bcsgj,bcsgjp->bcgjpn', bc, decay_to_end, xdt)
    chunk_decay = jnp.exp(acs[:, :, -1])

    def step(prev, inp):
        st, dec = inp
        return prev * dec[..., None, None] + st, prev

    init = jnp.zeros((b, g, j, p, n), xs.dtype)
    _, prev_states = lax.scan(step, init, (states.swapaxes(0, 1), chunk_decay.swapaxes(0, 1)))
    prev_states = prev_states.swapaxes(0, 1)
    y_off = jnp.einsum('bclgn,bcgjpn,bclgj->bclgjp', cc, prev_states, jnp.exp(acs))
    return (y_diag + y_off).reshape(b, t, h, p)


def ssd_branch(z, xbc, dt_raw, conv_w, conv_b, dt_bias, a_log, d_skip, norm_g):
    b, t, _ = xbc.shape
    xbc = lax.conv_general_dilated(
        xbc, conv_w, window_strides=(1,), padding=[(SSD_CONV - 1, 0)],
        dimension_numbers=('NWC', 'WIO', 'NWC'), feature_group_count=SSD_CONV_CH) + conv_b
    xbc = jax.nn.silu(xbc)
    xs, bm, cm = jnp.split(xbc, [SSD_D_INNER, SSD_D_INNER + SSD_GROUPS * SSD_STATE], axis=-1)
    xs = xs.reshape(b, t, SSD_HEADS, SSD_HEAD_DIM).astype(jnp.float32)
    bm = bm.reshape(b, t, SSD_GROUPS, SSD_STATE).astype(jnp.float32)
    cm = cm.reshape(b, t, SSD_GROUPS, SSD_STATE).astype(jnp.float32)
    dt = jax.nn.softplus(dt_raw.astype(jnp.float32) + dt_bias.astype(jnp.float32))
    a = -jnp.exp(a_log.astype(jnp.float32))
    y = ssd_chunked(xs, dt, a, bm, cm) + xs * d_skip.astype(jnp.float32)[:, None]
    y = y.reshape(b, t, SSD_D_INNER) * jax.nn.silu(z.astype(jnp.float32))
    return rmsnorm(y, norm_g).astype(z.dtype)


def memory_cross_attention(a, m, w_q, w_kv, w_o):
    b, t, _ = a.shape
    q = (a @ w_q).reshape(b, t, MEM_HEADS, MEM_HEAD_DIM)
    kk, vv = jnp.split(m @ w_kv, 2, axis=-1)
    kk = kk.reshape(b, m.shape[1], MEM_HEADS, MEM_HEAD_DIM)
    vv = vv.reshape(b, m.shape[1], MEM_HEADS, MEM_HEAD_DIM)
    logits = jnp.einsum('bthd,bmhd->bhtm', q, kk).astype(jnp.float32) * MEM_HEAD_DIM ** -0.5
    p = jax.nn.softmax(logits, axis=-1).astype(vv.dtype)
    o = jnp.einsum('bhtm,bmhd->bthd', p, vv).reshape(b, t, D_MODEL)
    return o @ w_o


def peer_ffn(a, w_q, sub_keys, u_tab, v_tab):
    b, t, d = a.shape
    n = b * t
    af = a.reshape(n, d)
    qry = (af @ w_q).reshape(n, PEER_HEADS, 2, PEER_KEY_DIM)
    s = jnp.einsum('nhpd,hpkd->nhpk', qry, sub_keys).astype(jnp.float32)
    s_top, i_top = lax.top_k(s, PEER_TOPK)
    cand = (s_top[:, :, 0, :, None] + s_top[:, :, 1, None, :]).reshape(n, PEER_HEADS, PEER_TOPK * PEER_TOPK)
    sc, ci = lax.top_k(cand, PEER_TOPK)
    e_a = jnp.take_along_axis(i_top[:, :, 0], ci // PEER_TOPK, axis=-1)
    e_b = jnp.take_along_axis(i_top[:, :, 1], ci % PEER_TOPK, axis=-1)
    experts = e_a * PEER_N_KEYS + e_b
    gw = jax.nn.softmax(sc, axis=-1)
    n_sel = PEER_HEADS * PEER_TOPK
    nb = n // PEER_TOKEN_BLOCK

    def block(args):
        xb, eb, gb = args
        u = u_tab[eb]
        act = jax.nn.gelu(jnp.einsum('td,tkd->tk', xb, u).astype(jnp.float32), approximate=False)
        wts = (gb * act).astype(v_tab.dtype)
        return jnp.einsum('tk,tkd->td', wts, v_tab[eb])

    out = lax.map(block, (af.reshape(nb, PEER_TOKEN_BLOCK, d),
                          experts.reshape(nb, PEER_TOKEN_BLOCK, n_sel),
                          gw.reshape(nb, PEER_TOKEN_BLOCK, n_sel)))
    return out.reshape(b, t, d)


def setup_inputs(seed: int = 0) -> dict:
    key = jax.random.key(seed)
    ks = jax.random.split(key, 26)
    f32 = jnp.float32
    L = DEPTH

    def nrm(k, shape, scale):
        return jax.random.normal(k, shape, f32) * scale

    def gain(k, shape):
        return 1.0 + 0.02 * jax.random.normal(k, shape, f32)

    dt0 = jnp.exp(jax.random.uniform(ks[6], (L, SSD_HEADS), f32, math.log(1e-3), math.log(1e-1)))
    return {
        "x": nrm(ks[0], (BATCH, SEQ, D_MODEL), 1.0),
        "mem": nrm(ks[1], (BATCH, MEM_LEN, D_MODEL), 1.0),
        "positions": jnp.broadcast_to(jnp.arange(SEQ, dtype=jnp.int32), (BATCH, SEQ)),
        "norm_mix_g": gain(ks[2], (L, D_MODEL)),
        "w_in": nrm(ks[3], (L, D_MODEL, IN_WIDTH), D_MODEL ** -0.5),
        "conv_w": nrm(ks[4], (L, SSD_CONV, 1, SSD_CONV_CH), SSD_CONV ** -0.5),
        "conv_b": nrm(ks[5], (L, SSD_CONV_CH), 0.02),
        "dt_bias": dt0 + jnp.log(-jnp.expm1(-dt0)),
        "a_log": jnp.log(jax.random.uniform(ks[7], (L, SSD_HEADS), f32, 1.0, 16.0)),
        "d_skip": gain(ks[8], (L, SSD_HEADS)),
        "ssd_norm_g": gain(ks[9], (L, SSD_D_INNER)),
        "w_attn_branch": nrm(ks[10], (L, ATTN_HEADS * ATTN_HEAD_DIM, D_MODEL), (ATTN_HEADS * ATTN_HEAD_DIM) ** -0.5),
        "w_ssd_branch": nrm(ks[11], (L, SSD_D_INNER, D_MODEL), SSD_D_INNER ** -0.5),
        "w_out": nrm(ks[12], (L, D_MODEL, D_MODEL), D_MODEL ** -0.5),
        "norm_cross_g": gain(ks[13], (L, D_MODEL)),
        "norm_mem_g": gain(ks[14], (L, D_MODEL)),
        "w_cross_q": nrm(ks[15], (L, D_MODEL, MEM_HEADS * MEM_HEAD_DIM), D_MODEL ** -0.5),
        "w_cross_kv": nrm(ks[16], (L, D_MODEL, 2 * MEM_HEADS * MEM_HEAD_DIM), D_MODEL ** -0.5),
        "w_cross_out": nrm(ks[17], (L, MEM_HEADS * MEM_HEAD_DIM, D_MODEL), D_MODEL ** -0.5),
        "norm_ffn_g": gain(ks[18], (L, D_MODEL)),
        "w_peer_q": nrm(ks[19], (L, D_MODEL, PEER_HEADS * 2 * PEER_KEY_DIM), D_MODEL ** -0.5),
        "peer_sub_keys": nrm(ks[20], (L, PEER_HEADS, 2, PEER_N_KEYS, PEER_KEY_DIM), PEER_KEY_DIM ** -0.5),
        "peer_u": nrm(ks[21], (L, PEER_N_EXPERTS, D_MODEL), D_MODEL ** -0.5),
        "peer_v": nrm(ks[22], (L, PEER_N_EXPERTS, D_MODEL), 0.3),
        "norm_final_g": gain(ks[23], (D_MODEL,)),
    }


def reference(x, mem, positions, norm_mix_g, w_in, conv_w, conv_b, dt_bias, a_log, d_skip,
              ssd_norm_g, w_attn_branch, w_ssd_branch, w_out, norm_cross_g, norm_mem_g,
              w_cross_q, w_cross_kv, w_cross_out, norm_ffn_g, w_peer_q, peer_sub_keys,
              peer_u, peer_v, norm_final_g):
    b, t, _ = x.shape
    offsets = _in_offsets()
    h = x
    for layer in range(DEPTH):
        a = rmsnorm(h, norm_mix_g[layer])
        proj = a @ w_in[layer]
        q, k, v, iq, ik, iw, z, xbc, dt_raw, gate = jnp.split(proj, offsets, axis=-1)
        q = partial_rope(q.reshape(b, t, ATTN_HEADS, ATTN_HEAD_DIM), positions)
        k = partial_rope(k.reshape(b, t, ATTN_KV_HEADS, ATTN_HEAD_DIM), positions)
        v = v.reshape(b, t, ATTN_KV_HEADS, ATTN_HEAD_DIM)
        iq = partial_rope(iq.reshape(b, t, IDX_HEADS, IDX_HEAD_DIM), positions)
        ik = partial_rope(ik.reshape(b, t, 1, IDX_HEAD_DIM), positions)[:, :, 0]
        attn = dsa_attention(q, k, v, iq, ik, iw)
        ssd = ssd_branch(z, xbc, dt_raw, conv_w[layer], conv_b[layer], dt_bias[layer],
                         a_log[layer], d_skip[layer], ssd_norm_g[layer])
        gates = jax.nn.sigmoid(gate.astype(jnp.float32)).reshape(b, t, N_BRANCHES, D_MODEL).astype(x.dtype)
        merged = (gates[:, :, 0] * (attn @ w_attn_branch[layer])
                  + gates[:, :, 1] * (ssd @ w_ssd_branch[layer]))
        h = h + merged @ w_out[layer]
        h = h + memory_cross_attention(rmsnorm(h, norm_cross_g[layer]), rmsnorm(mem, norm_mem_g[layer]),
                                       w_cross_q[layer], w_cross_kv[layer], w_cross_out[layer])
        h = h + peer_ffn(rmsnorm(h, norm_ffn_g[layer]), w_peer_q[layer], peer_sub_keys[layer],
                         peer_u[layer], peer_v[layer])
    return rmsnorm(h, norm_final_g)
```

```python
import functools
import math

import jax
import jax.numpy as jnp
from jax import lax
from jax.experimental import pallas as pl
from jax.experimental.pallas import tpu as pltpu

F32 = jnp.float32
BF16 = jnp.bfloat16
I32 = jnp.int32

ATTN_HEADS = 8
ATTN_KV_HEADS = 2
ATTN_HEAD_DIM = 128
IDX_HEADS = 4
IDX_HEAD_DIM = 64
TOPK_MAX = 256
ROPE_THETA = 500000.0
ROPE_FRACTION = 4
SSD_HEAD_DIM = 64
SSD_GROUPS = 4
SSD_STATE = 128
SSD_CONV = 4
SSD_CHUNK = 128
MEM_HEADS = 4
PEER_HEADS = 8
PEER_N_KEYS = 128
PEER_KEY_DIM = 128
PEER_TOPK = 16
EPS = 1e-6

LANES = 128
SUBLANES = 8
VMEM_LIMIT = 56 * 1024 * 1024

NEG_BIG = -1e30
INT_MIN = -2 ** 31


def _cparams(*sem):
    return pltpu.CompilerParams(dimension_semantics=sem, vmem_limit_bytes=VMEM_LIMIT)


def _const_spec(shape):
    nd = len(shape)
    return pl.BlockSpec(shape, lambda *_: (0,) * nd, pipeline_mode=pl.Buffered(1))


def _rmsnorm(x, g):
    return x * lax.rsqrt(jnp.mean(x * x, axis=-1, keepdims=True) + EPS) * g


def _split3(a):
    hi = a.astype(BF16)
    r1 = a - hi.astype(F32)
    mid = r1.astype(BF16)
    lo = (r1 - mid.astype(F32)).astype(BF16)
    return hi, mid, lo


def _dot_exact_rhs(a, b_bf16):
    hi, mid, lo = _split3(a)
    d = functools.partial(jnp.dot, preferred_element_type=F32)
    return d(hi, b_bf16) + d(mid, b_bf16) + d(lo, b_bf16)


def _dot_exact_lhs(a_bf16, b):
    hi, mid, lo = _split3(b)
    d = functools.partial(jnp.dot, preferred_element_type=F32)
    return d(a_bf16, hi) + d(a_bf16, mid) + d(a_bf16, lo)


def _dot_nt(a, b):
    return lax.dot_general(a, b, (((1,), (1,)), ((), ())), preferred_element_type=F32)


def _rope(x, cos, sin_signed, first_half, half):
    w = x.shape[-1]
    x_sw = jnp.where(first_half, pltpu.roll(x, w - half, 1), pltpu.roll(x, half, 1))
    return x * cos + x_sw * sin_signed


def _attn_proj_kernel(x_ref, g_ref, w_ref, pos_ref, rope_ref,
                      q_ref, k_ref, v_ref, iq_ref, ik_ref, iw_ref):
    a = _rmsnorm(x_ref[...], g_ref[...]).astype(BF16)
    p = jnp.dot(a, w_ref[...], preferred_element_type=F32)
    pos = pos_ref[...]
    tabs = []
    for r, half in ((0, ATTN_HEAD_DIM // ROPE_FRACTION // 2), (2, IDX_HEAD_DIM // ROPE_FRACTION // 2)):
        ang = pos * rope_ref[r:r + 1, :]
        sgn = rope_ref[r + 1:r + 2, :]
        tabs.append((jnp.cos(ang), jnp.sin(ang) * sgn, sgn < 0.0, half))
    t128, t64 = tabs
    off = 0
    for h in range(ATTN_HEADS):
        q_ref[:, h * LANES:(h + 1) * LANES] = _rope(p[:, off:off + LANES], *t128).astype(BF16)
        off += LANES
    for h in range(ATTN_KV_HEADS):
        k_ref[:, h * LANES:(h + 1) * LANES] = _rope(p[:, off:off + LANES], *t128).astype(BF16)
        off += LANES
    v_ref[...] = p[:, off:off + ATTN_KV_HEADS * LANES].astype(BF16)
    off += ATTN_KV_HEADS * LANES
    for h in range(IDX_HEADS):
        iq_ref[:, h * LANES:(h + 1) * LANES] = _rope(p[:, off:off + LANES], *t64).astype(BF16)
        off += LANES
    ik_ref[...] = _rope(p[:, off:off + LANES], *t64).astype(BF16)
    off += LANES
    iw_ref[...] = p[:, off:off + LANES]


def _rope_rows():
    rows = []
    for head_dim in (ATTN_HEAD_DIM, IDX_HEAD_DIM):
        rot = head_dim // ROPE_FRACTION
        half = rot // 2
        inv = ROPE_THETA ** (-2.0 * jnp.arange(half, dtype=F32) / rot)
        inv_row = jnp.zeros((LANES,), F32).at[:rot].set(jnp.concatenate([inv, inv]))
        sgn_row = jnp.zeros((LANES,), F32).at[:half].set(-1.0).at[half:rot].set(1.0)
        rows += [inv_row, sgn_row]
    return jnp.stack(rows)


def _pad_cols(w, width):
    return jnp.pad(w, ((0, 0), (0, width - w.shape[1])))


def _attn_proj(x2, g, w_in, pos_f32, tm):
    n, d = x2.shape
    hq = ATTN_HEADS * ATTN_HEAD_DIM
    hkv = ATTN_KV_HEADS * ATTN_HEAD_DIM
    o = 0
    wq = w_in[:, o:o + hq]; o += hq
    wk = w_in[:, o:o + hkv]; o += hkv
    wv = w_in[:, o:o + hkv]; o += hkv
    wiq = w_in[:, o:o + IDX_HEADS * IDX_HEAD_DIM]; o += IDX_HEADS * IDX_HEAD_DIM
    wik = w_in[:, o:o + IDX_HEAD_DIM]; o += IDX_HEAD_DIM
    wiw = w_in[:, o:o + IDX_HEADS]; o += IDX_HEADS
    wiq = jnp.pad(wiq.reshape(d, IDX_HEADS, IDX_HEAD_DIM),
                  ((0, 0), (0, 0), (0, LANES - IDX_HEAD_DIM))).reshape(d, IDX_HEADS * LANES)
    w = jnp.concatenate([wq, wk, wv, wiq, _pad_cols(wik, LANES), _pad_cols(wiw, LANES)], axis=1).astype(BF16)
    nw = w.shape[1]
    row = lambda width: pl.BlockSpec((tm, width), lambda i: (i, 0))
    outs = pl.pallas_call(
        _attn_proj_kernel,
        grid=(n // tm,),
        in_specs=[row(d), _const_spec((1, d)), _const_spec((d, nw)), row(1), _const_spec((4, LANES))],
        out_specs=[row(hq), row(hkv), row(hkv), row(IDX_HEADS * LANES), row(LANES), row(LANES)],
        out_shape=[jax.ShapeDtypeStruct((n, hq), BF16), jax.ShapeDtypeStruct((n, hkv), BF16),
                   jax.ShapeDtypeStruct((n, hkv), BF16), jax.ShapeDtypeStruct((n, IDX_HEADS * LANES), BF16),
                   jax.ShapeDtypeStruct((n, LANES), BF16), jax.ShapeDtypeStruct((n, LANES), F32)],
        compiler_params=_cparams("parallel"),
        name="attn_proj",
    )(x2, g, w, pos_f32, _rope_rows())
    return outs


def _ssd_proj_kernel(x_ref, g_ref, w_ref, z_ref, xbc_ref, dt_ref, *, dz, dxbc):
    a = _rmsnorm(x_ref[...], g_ref[...]).astype(BF16)
    z_ref[...] = jnp.dot(a, w_ref[:, :dz], preferred_element_type=F32)
    xbc_ref[...] = jnp.dot(a, w_ref[:, dz:dz + dxbc], preferred_element_type=F32)
    dt_ref[...] = jnp.dot(a, w_ref[:, dz + dxbc:], preferred_element_type=F32)


def _gate_proj_kernel(x_ref, g_ref, w_ref, o_ref):
    a = _rmsnorm(x_ref[...], g_ref[...]).astype(BF16)
    o_ref[...] = jax.nn.sigmoid(jnp.dot(a, w_ref[...], preferred_element_type=F32))


def _plain_proj_kernel(x_ref, g_ref, w_ref, o_ref):
    a = _rmsnorm(x_ref[...], g_ref[...]).astype(BF16)
    o_ref[...] = jnp.dot(a, w_ref[...], preferred_element_type=F32).astype(o_ref.dtype)


def _rms_proj(body, x2, g, w, out_dtype, tm, name):
    n, d = x2.shape
    nw = w.shape[1]
    return pl.pallas_call(
        body,
        grid=(n // tm,),
        in_specs=[pl.BlockSpec((tm, d), lambda i: (i, 0)), _const_spec((1, d)), _const_spec((d, nw))],
        out_specs=pl.BlockSpec((tm, nw), lambda i: (i, 0)),
        out_shape=jax.ShapeDtypeStruct((n, nw), out_dtype),
        compiler_params=_cparams("parallel"),
        name=name,
    )(x2, g, w)


def _ssd_proj(x2, g, w_z, w_xbc, w_dt, tm):
    n, d = x2.shape
    dz, dxbc = w_z.shape[1], w_xbc.shape[1]
    w = jnp.concatenate([w_z, w_xbc, _pad_cols(w_dt, LANES)], axis=1).astype(BF16)
    row = lambda width: pl.BlockSpec((tm, width), lambda i: (i, 0))
    return pl.pallas_call(
        functools.partial(_ssd_proj_kernel, dz=dz, dxbc=dxbc),
        grid=(n // tm,),
        in_specs=[row(d), _const_spec((1, d)), _const_spec((d, w.shape[1]))],
        out_specs=[row(dz), row(dxbc), row(LANES)],
        out_shape=[jax.ShapeDtypeStruct((n, dz), F32), jax.ShapeDtypeStruct((n, dxbc), F32),
                   jax.ShapeDtypeStruct((n, LANES), F32)],
        compiler_params=_cparams("parallel"),
        name="ssd_proj",
    )(x2, g, w)


DSA_QB = 128
DSA_KC = 512


def _dsa_kernel(iq_ref, iw_ref, q_ref, ik_ref, k_ref, v_ref, tri_ref, o_ref,
                key_ref, m_ref, l_ref, acc_ref, *, nsel):
    qb = pl.program_id(1)
    nchunk = qb // (DSA_KC // DSA_QB) + 1
    grp = ATTN_HEADS // ATTN_KV_HEADS
    idx_scale = IDX_HEAD_DIM ** -0.5
    w_scale = IDX_HEADS ** -0.5
    att_scale = ATTN_HEAD_DIM ** -0.5

    t_glob = qb * DSA_QB + lax.broadcasted_iota(I32, (DSA_KC, DSA_QB), 1)
    s_loc = lax.broadcasted_iota(I32, (DSA_KC, DSA_QB), 0)
    iw_t = jnp.transpose(iw_ref[...]) * w_scale

    def score_chunk(c, carry):
        ikc = ik_ref[pl.ds(c * DSA_KC, DSA_KC), :]
        sc = jnp.zeros((DSA_KC, DSA_QB), F32)
        for h in range(IDX_HEADS):
            d = _dot_nt(ikc, iq_ref[:, h * LANES:(h + 1) * LANES])
            sc = sc + jnp.maximum(d * idx_scale, 0.0) * iw_t[h:h + 1, :]
        sc = jnp.where(c * DSA_KC + s_loc <= t_glob, sc, -jnp.inf)
        bits = pltpu.bitcast(sc, I32)
        key_ref[pl.ds(c * DSA_KC, DSA_KC), :] = jnp.where(bits < 0, bits ^ 0x7FFFFFFF, bits)
        return carry

    lax.fori_loop(0, nchunk, score_chunk, 0)

    def count_ge(cand):
        def body(c, acc):
            for j in range(DSA_KC // SUBLANES):
                blk = key_ref[pl.ds(c * DSA_KC + j * SUBLANES, SUBLANES), :]
                acc = acc + jnp.where(blk >= cand, 1, 0)
            return acc
        acc = lax.fori_loop(0, nchunk, body, jnp.zeros((SUBLANES, DSA_QB), I32))
        return jnp.sum(acc, axis=0, keepdims=True)

    def bisect(it, ans):
        cand = ans + lax.shift_left(jnp.int32(1), 31 - it)
        cnt = count_ge(cand)
        return jnp.where(cnt >= nsel, cand, ans)

    tau = lax.fori_loop(0, 32, bisect, jnp.full((SUBLANES, DSA_QB), INT_MIN, I32))
    n_gt = count_ge(tau + 1)
    n_tie = (nsel - n_gt).astype(F32)
    tau_row = tau[0:1, :]

    m_ref[...] = jnp.full(m_ref.shape, NEG_BIG, F32)
    l_ref[...] = jnp.zeros(l_ref.shape, F32)
    acc_ref[...] = jnp.zeros(acc_ref.shape, F32)

    def attn_chunk(c, tie_carry):
        keys = key_ref[pl.ds(c * DSA_KC, DSA_KC), :]
        eq = keys == tau_row
        prefix = jnp.dot(tri_ref[...], jnp.where(eq, 1.0, 0.0).astype(BF16),
                         preferred_element_type=F32) + tie_carry
        sel = (keys > tau_row) | (eq & (prefix <= n_tie))
        sel = sel & (c * DSA_KC + s_loc <= t_glob)
        bias = jnp.transpose(jnp.where(sel, 0.0, NEG_BIG))
        bias = jnp.concatenate([bias] * grp, axis=0)
        for kvh in range(ATTN_KV_HEADS):
            kc = k_ref[pl.ds(c * DSA_KC, DSA_KC), kvh * LANES:(kvh + 1) * LANES]
            vc = v_ref[pl.ds(c * DSA_KC, DSA_KC), kvh * LANES:(kvh + 1) * LANES]
            qs = jnp.concatenate([q_ref[:, (kvh * grp + g) * LANES:(kvh * grp + g + 1) * LANES]
                                  for g in range(grp)], axis=0)
            logits = _dot_nt(qs, kc) * att_scale + bias
            m_old = m_ref[kvh]
            m_new = jnp.maximum(m_old, jnp.max(logits, axis=-1, keepdims=True))
            p = jnp.exp(logits - m_new)
            alpha = jnp.exp(m_old - m_new)
            l_ref[kvh] = alpha * l_ref[kvh] + jnp.sum(p, axis=-1, keepdims=True)
            acc_ref[kvh] = alpha * acc_ref[kvh] + jnp.dot(p.astype(BF16), vc, preferred_element_type=F32)
            m_ref[kvh] = m_new
        return prefix[DSA_KC - 1:DSA_KC, :]

    lax.fori_loop(0, nchunk, attn_chunk, jnp.zeros((1, DSA_QB), F32))
    for kvh in range(ATTN_KV_HEADS):
        o = acc_ref[kvh] / l_ref[kvh]
        for g in range(grp):
            h = kvh * grp + g
            o_ref[:, h * LANES:(h + 1) * LANES] = o[g * DSA_QB:(g + 1) * DSA_QB, :].astype(o_ref.dtype)


def _dsa(q, k, v, iq, ik, iw, batch, seq):
    n = batch * seq
    nsel = min(TOPK_MAX, seq // 4)
    nqb = seq // DSA_QB
    grp_rows = ATTN_HEADS // ATTN_KV_HEADS * DSA_QB
    tri =(jnp.arange(DSA_KC)[:, None] >= jnp.arange(DSA_KC)[None, :]).astype(BF16)
    qrow = lambda width: pl.BlockSpec((DSA_QB, width), lambda b, i: (b * nqb + i, 0))
    full = lambda width: pl.BlockSpec((seq, width), lambda b, i: (b, 0))
    return pl.pallas_call(
        functools.partial(_dsa_kernel, nsel=nsel),
        grid=(batch, nqb),
        in_specs=[qrow(IDX_HEADS * LANES), qrow(LANES), qrow(ATTN_HEADS * ATTN_HEAD_DIM),
                  full(LANES), full(ATTN_KV_HEADS * ATTN_HEAD_DIM), full(ATTN_KV_HEADS * ATTN_HEAD_DIM),
                  _const_spec((DSA_KC, DSA_KC))],
        out_specs=qrow(ATTN_HEADS * ATTN_HEAD_DIM),
        out_shape=jax.ShapeDtypeStruct((n, ATTN_HEADS * ATTN_HEAD_DIM), BF16),
        scratch_shapes=[pltpu.VMEM((seq, DSA_QB), I32),
                        pltpu.VMEM((ATTN_KV_HEADS, grp_rows, 1), F32),
                        pltpu.VMEM((ATTN_KV_HEADS, grp_rows, 1), F32),
                        pltpu.VMEM((ATTN_KV_HEADS, grp_rows, ATTN_HEAD_DIM), F32)],
        compiler_params=_cparams("arbitrary", "arbitrary"),
        name="dsa",
    )(iq, iw, q, ik, k, v, tri)


CONV_HALO = 8


def _ssd_kernel(xbc_ref, z_ref, dt_ref, cw_ref, cb_ref, dtb_ref, alog_ref, dskip_ref, ng_ref,
                tri_ref, exp_ref, o_ref, ext_ref, state_ref, *, d_inner):
    L = SSD_CHUNK
    gn = SSD_GROUPS * SSD_STATE
    heads_per_group = d_inner // SSD_HEAD_DIM // SSD_GROUPS
    gw = heads_per_group * SSD_HEAD_DIM

    @pl.when(pl.program_id(1) == 0)
    def _():
        ext_ref[0:CONV_HALO, :] = jnp.zeros((CONV_HALO, ext_ref.shape[1]), F32)
        state_ref[...] = jnp.zeros(state_ref.shape, F32)

    raw = xbc_ref[...]
    ext_ref[CONV_HALO:CONV_HALO + L, :] = raw
    acc = cb_ref[...] + jnp.zeros_like(raw)
    for kk in range(SSD_CONV):
        start = CONV_HALO - (SSD_CONV - 1) + kk
        acc = acc + cw_ref[kk:kk + 1, :] * ext_ref[start:start + L, :]
    ext_ref[0:CONV_HALO, :] = raw[L - CONV_HALO:L, :]
    xbc = acc * jax.nn.sigmoid(acc)
    xs = xbc[:, :d_inner]
    bm = xbc[:, d_inner:d_inner + gn]
    cm = xbc[:, d_inner + gn:]

    dt = jax.nn.softplus(dt_ref[...] + dtb_ref[...])
    a = -jnp.exp(alog_ref[...])
    acs = _dot_exact_lhs(tri_ref[...], dt * a)
    acs_t = jnp.transpose(acs)
    e_acs = jnp.exp(acs)
    e_end = jnp.exp(acs[L - 1:L, :] - acs)
    expand = exp_ref[...]
    dt_x = _dot_exact_rhs(dt, expand)
    e_acs_x = _dot_exact_rhs(e_acs, expand)
    e_end_x = _dot_exact_rhs(e_end, expand)
    xdt = xs * dt_x
    xdt_b = xdt.astype(BF16)
    xw_b = (xdt * e_end_x).astype(BF16)

    lower = lax.broadcasted_iota(I32, (L, L), 0) >= lax.broadcasted_iota(I32, (L, L), 1)
    lane = lax.broadcasted_iota(I32, (L, LANES), 1)
    heads_per_tile = LANES // SSD_HEAD_DIM
    for g in range(SSD_GROUPS):
        cg = cm[:, g * SSD_STATE:(g + 1) * SSD_STATE].astype(BF16)
        bg = bm[:, g * SSD_STATE:(g + 1) * SSD_STATE]
        cb = _dot_nt(cg, bg.astype(BF16))
        st = state_ref[:, g * gw:(g + 1) * gw]
        y_off = jnp.dot(cg, st.astype(BF16), preferred_element_type=F32) * e_acs_x[:, g * gw:(g + 1) * gw]
        for tile in range(gw // LANES):
            col = g * gw + tile * LANES
            x_tile = xdt_b[:, col:col + LANES]
            y_tile = y_off[:, tile * LANES:(tile + 1) * LANES]
            for sub in range(heads_per_tile):
                h = col // SSD_HEAD_DIM + sub
                seg = acs[:, h:h + 1] - acs_t[h:h + 1, :]
                m = (cb * jnp.exp(jnp.where(lower, seg, -jnp.inf))).astype(BF16)
                in_head = (lane >= sub * SSD_HEAD_DIM) & (lane < (sub + 1) * SSD_HEAD_DIM)
                y_tile = y_tile + jnp.dot(m, jnp.where(in_head, x_tile, jnp.zeros_like(x_tile)),
                                          preferred_element_type=F32)
            y_tile = y_tile + xs[:, col:col + LANES] * dskip_ref[:, col:col + LANES]
            zt = z_ref[:, col:col + LANES]
            o_ref[:, col:col + LANES] = (y_tile * (zt * jax.nn.sigmoid(zt))).astype(o_ref.dtype)
        bg_t = jnp.transpose(bg).astype(BF16)
        state_ref[:, g * gw:(g + 1) * gw] = (
            st * e_acs_x[L - 1:L, g * gw:(g + 1) * gw]
            + jnp.dot(bg_t, xw_b[:, g * gw:(g + 1) * gw], preferred_element_type=F32))

    y = o_ref[...].astype(F32)
    o_ref[...] = _rmsnorm(y, ng_ref[...]).astype(o_ref.dtype)


def _ssd(z, xbc, dt, conv_w, conv_b, dt_bias, a_log, d_skip, norm_g, batch, seq):
    n, d_inner = z.shape
    cch = xbc.shape[1]
    heads = d_inner // SSD_HEAD_DIM
    nc = seq // SSD_CHUNK
    L = SSD_CHUNK
    tri = (jnp.arange(L)[:, None] >= jnp.arange(L)[None, :]).astype(BF16)
    expand = (jnp.arange(LANES)[:, None] == (jnp.arange(d_inner)[None, :] // SSD_HEAD_DIM)).astype(BF16)
    pad_h = lambda v: jnp.pad(v.reshape(1, heads), ((0, 0), (0, LANES - heads)))
    row = lambda width: pl.BlockSpec((L, width), lambda b, c: (b * nc + c, 0))
    return pl.pallas_call(
        functools.partial(_ssd_kernel, d_inner=d_inner),
        grid=(batch, nc),
        in_specs=[row(cch), row(d_inner), row(LANES),
                  _const_spec((SSD_CONV, cch)), _const_spec((1, cch)), _const_spec((1, LANES)),
                  _const_spec((1, LANES)), _const_spec((1, d_inner)), _const_spec((1, d_inner)),
                  _const_spec((L, L)), _const_spec((LANES, d_inner))],
        out_specs=row(d_inner),
        out_shape=jax.ShapeDtypeStruct((n, d_inner), F32),
        scratch_shapes=[pltpu.VMEM((CONV_HALO + L, cch), F32), pltpu.VMEM((SSD_STATE, d_inner), F32)],
        compiler_params=_cparams("arbitrary", "arbitrary"),
        name="ssd",
    )(xbc, z, dt, conv_w.reshape(SSD_CONV, cch), conv_b.reshape(1, cch), pad_h(dt_bias), pad_h(a_log),
      jnp.repeat(d_skip, SSD_HEAD_DIM).reshape(1, d_inner), norm_g.reshape(1, d_inner), tri, expand)


def _merge_kernel(x_ref, attn_ref, ssd_ref, gate_ref, wab_ref, wsb_ref, wout_ref, gc_ref, wcq_ref,
                  kv_ref, wco_ref, gf_ref, h_ref, a_ref, *, d_model):
    dot = functools.partial(jnp.dot, preferred_element_type=F32)
    br_a = dot(attn_ref[...], wab_ref[...])
    br_s = dot(ssd_ref[...].astype(BF16), wsb_ref[...])
    merged = gate_ref[:, :d_model] * br_a + gate_ref[:, d_model:] * br_s
    h = x_ref[...] + dot(merged.astype(BF16), wout_ref[...])

    qc = dot(_rmsnorm(h, gc_ref[...]).astype(BF16), wcq_ref[...]).astype(BF16)
    dh = d_model // MEM_HEADS
    outs = []
    for hd in range(MEM_HEADS):
        kk = kv_ref[:, hd * dh:(hd + 1) * dh]
        vv = kv_ref[:, d_model + hd * dh:d_model + (hd + 1) * dh]
        logits = _dot_nt(qc[:, hd * dh:(hd + 1) * dh], kk) * dh ** -0.5
        e = jnp.exp(logits - jnp.max(logits, axis=-1, keepdims=True))
        p = e / jnp.sum(e, axis=-1, keepdims=True)
        outs.append(dot(p.astype(BF16), vv))
    o = jnp.concatenate(outs, axis=-1)
    h = h + dot(o.astype(BF16), wco_ref[...])
    h_ref[...] = h
    a_ref[...] = _rmsnorm(h, gf_ref[...]).astype(a_ref.dtype)


def _merge(x2, attn, ssd, gates, w_ab, w_sb, w_out, g_cross, w_cq, kv, w_co, g_ffn, batch, seq, tm):
    n, d = x2.shape
    mem_len = kv.shape[0] // batch
    per_b = seq // tm
    row = lambda width: pl.BlockSpec((tm, width), lambda i: (i, 0))
    return pl.pallas_call(
        functools.partial(_merge_kernel, d_model=d),
        grid=(n // tm,),
        in_specs=[row(d), row(attn.shape[1]), row(ssd.shape[1]), row(gates.shape[1]),
                  _const_spec(w_ab.shape), _const_spec(w_sb.shape), _const_spec(w_out.shape),
                  _const_spec((1, d)), _const_spec(w_cq.shape),
                  pl.BlockSpec((mem_len, kv.shape[1]), lambda i: (i // per_b, 0)),
                  _const_spec(w_co.shape), _const_spec((1, d))],
        out_specs=[row(d), row(d)],
        out_shape=[jax.ShapeDtypeStruct((n, d), F32), jax.ShapeDtypeStruct((n, d), BF16)],
        compiler_params=_cparams("parallel"),
        name="merge",
    )(x2, attn, ssd, gates, w_ab, w_sb, w_out, g_cross, w_cq, kv, w_co, g_ffn)


def _top16_rows(s_t):
    nk, tn = s_t.shape
    key_iota = lax.broadcasted_iota(I32, (nk, tn), 0).astype(F32)
    slot_iota = lax.broadcasted_iota(I32, (PEER_TOPK, tn), 0)
    cur = s_t
    rank = jnp.full((nk, tn), float(PEER_TOPK), F32)
    vals = jnp.zeros((PEER_TOPK, tn), F32)
    for r in range(PEER_TOPK):
        m = jnp.max(cur, axis=0, keepdims=True)
        first = jnp.min(jnp.where(cur == m, key_iota, float(nk)), axis=0, keepdims=True)
        hit = key_iota == first
        rank = jnp.where(hit, float(r), rank)
        cur = jnp.where(hit, -jnp.inf, cur)
        vals = jnp.where(slot_iota == r, m, vals)
    return vals, rank


def _pair_merge(v1, v2):
    k, tn = v1.shape
    slot = lax.broadcasted_iota(I32, (k, tn), 0).astype(F32)
    slot_f = slot
    count = jnp.zeros((k, tn), F32)
    front = v1 + v2[0:1, :]
    best = front[0:1, :]
    z = jnp.zeros((1, tn), F32)
    for _ in range(PEER_TOPK):
        m = jnp.max(front, axis=0, keepdims=True)
        first = jnp.min(jnp.where(front == m, slot, float(k)), axis=0, keepdims=True)
        hit = slot == first
        z = z + jnp.exp(m - best)
        count = jnp.where(hit, count + 1.0, count)
        nxt = jnp.sum(jnp.where(hit, count, 0.0), axis=0, keepdims=True)
        v2n = jnp.sum(jnp.where(slot_f == nxt, v2, 0.0), axis=0, keepdims=True)
        v2n = jnp.where(nxt >= float(k), -jnp.inf, v2n)
        front = jnp.where(hit, v1 + v2n, front)
    return count, z


def _peer_route_kernel(a_ref, wq_ref, sk_ref, r2_ref, g2_ref, la_ref, c1_ref):
    qry = jnp.dot(a_ref[...], wq_ref[...], preferred_element_type=F32).astype(BF16)
    for h in range(PEER_HEADS):
        halves = []
        for p in range(2):
            hp = h * 2 + p
            s_t = _dot_nt(sk_ref[hp], qry[:, hp * PEER_KEY_DIM:(hp + 1) * PEER_KEY_DIM])
            vals, rank = _top16_rows(s_t)
            halves.append((s_t, vals, rank))
        (s1, v1, rank1), (s2, v2, rank2) = halves
        count, z = _pair_merge(v1, v2)
        la = jnp.zeros_like(rank1)
        for i in range(PEER_TOPK):
            la = la + jnp.where(rank1 == float(i), count[i:i + 1, :], 0.0)
        r2_ref[h] = rank2
        g2_ref[h] = jnp.exp(s2 - v2[0:1, :]) / z
        la_ref[h] = la
        c1_ref[h] = jnp.exp(s1 - v1[0:1, :])


def _peer_route(a3, w_pq, sub_keys, tn):
    n, d = a3.shape
    nk = sub_keys.shape[1]
    tab = jax.ShapeDtypeStruct((PEER_HEADS, nk, n), F32)
    tspec = pl.BlockSpec((PEER_HEADS, nk, tn), lambda i: (0, 0, i))
    return pl.pallas_call(
        _peer_route_kernel,
        grid=(n // tn,),
        in_specs=[pl.BlockSpec((tn, d), lambda i: (i, 0)), _const_spec(w_pq.shape), _const_spec(sub_keys.shape)],
        out_specs=[tspec] * 4,
        out_shape=[tab] * 4,
        compiler_params=_cparams("parallel"),
        name="peer_route",
    )(a3, w_pq, sub_keys)


def _gelu_exact(x):
    return 0.5 * x * (1.0 + lax.erf(x * (2.0 ** -0.5)))


def _peer_dense_kernel(at_ref, u_ref, vt_ref, r2_ref, g2_ref, la_ref, c1_ref, h_ref, gfin_ref,
                       o_ref, yt_ref, *, na, final_norm):
    j = pl.program_id(1)

    @pl.when(j == 0)
    def _():
        yt_ref[...] = jnp.zeros(yt_ref.shape, F32)

    nk = PEER_N_KEYS
    z_t = jnp.dot(u_ref[...], at_ref[...], preferred_element_type=F32)
    parts = []
    for ai in range(na):
        a_idx = j * na + ai
        w = jnp.zeros((nk, z_t.shape[1]), F32)
        for h in range(PEER_HEADS):
            la = la_ref[h, pl.ds(a_idx, 1), :]
            c1 = c1_ref[h, pl.ds(a_idx, 1), :]
            w = w + jnp.where(r2_ref[h] < la, g2_ref[h], 0.0) * c1
        parts.append((w * _gelu_exact(z_t[ai * nk:(ai + 1) * nk, :])).astype(BF16))
    p_t = jnp.concatenate(parts, axis=0)
    yt_ref[...] += jnp.dot(vt_ref[...], p_t, preferred_element_type=F32)

    @pl.when(j == pl.num_programs(1) - 1)
    def _():
        h = h_ref[...] + jnp.transpose(yt_ref[...])
        o_ref[...] = _rmsnorm(h, gfin_ref[...]) if final_norm else h


def _peer_dense(a3_t, u_bf, v_t, tabs, h2, g_final, tn, na, final_norm):
    d, n = a3_t.shape
    n_exp = u_bf.shape[0]
    nk = PEER_N_KEYS
    tspec = pl.BlockSpec((PEER_HEADS, nk, tn), lambda i, j: (0, 0, i))
    return pl.pallas_call(
        functools.partial(_peer_dense_kernel, na=na, final_norm=final_norm),
        grid=(n // tn, n_exp // (na * nk)),
        in_specs=[pl.BlockSpec((d, tn), lambda i, j: (0, i)),
                  pl.BlockSpec((na * nk, d), lambda i, j: (j, 0)),
                  pl.BlockSpec((d, na * nk), lambda i, j: (0, j)),
                  tspec, tspec, tspec, tspec,
                  pl.BlockSpec((tn, d), lambda i, j: (i, 0)),
                  _const_spec((1, d))],
        out_specs=pl.BlockSpec((tn, d), lambda i, j: (i, 0)),
        out_shape=jax.ShapeDtypeStruct((n, d), F32),
        scratch_shapes=[pltpu.VMEM((d, tn), F32)],
        compiler_params=_cparams("parallel", "arbitrary"),
        name="peer_dense",
    )(a3_t, u_bf, v_t, *tabs, h2, g_final)


def _split_w_in(w_in, d_model, d_inner, conv_ch, ssd_heads):
    sizes = (ATTN_HEADS * ATTN_HEAD_DIM, ATTN_KV_HEADS * ATTN_HEAD_DIM, ATTN_KV_HEADS * ATTN_HEAD_DIM,
             IDX_HEADS * IDX_HEAD_DIM, IDX_HEAD_DIM, IDX_HEADS, d_inner, conv_ch, ssd_heads, 2 * d_model)
    assert sum(sizes) == w_in.shape[1]
    attn_w = sum(sizes[:6])
    o = attn_w
    w_z = w_in[:, o:o + d_inner]; o += d_inner
    w_xbc = w_in[:, o:o + conv_ch]; o += conv_ch
    w_dt = w_in[:, o:o + ssd_heads]; o += ssd_heads
    w_gate = w_in[:, o:]
    return w_in[:, :attn_w], w_z, w_xbc, w_dt, w_gate


def kernel(x, mem, positions, norm_mix_g, w_in, conv_w, conv_b, dt_bias, a_log, d_skip, ssd_norm_g, w_attn_branch, w_ssd_branch, w_out, norm_cross_g, norm_mem_g, w_cross_q, w_cross_kv, w_cross_out, norm_ffn_g, w_peer_q, peer_sub_keys, peer_u, peer_v, norm_final_g):
    batch, seq, d = x.shape
    n = batch * seq
    depth = w_in.shape[0]
    d_inner = ssd_norm_g.shape[1]
    conv_ch = conv_b.shape[1]
    ssd_heads = dt_bias.shape[1]
    tm = 256
    row = lambda v: v.reshape(1, -1)

    h = x.reshape(n, d)
    pos = positions.reshape(n, 1).astype(F32)
    mem2 = mem.reshape(batch * mem.shape[1], d)
    a3 = tabs = None
    for layer in range(depth):
        w_attn, w_z, w_xbc, w_dt, w_gate = _split_w_in(w_in[layer], d, d_inner, conv_ch, ssd_heads)
        g_mix = row(norm_mix_g[layer])
        q, k, v, iq, ik, iw = _attn_proj(h, g_mix, w_attn, pos, tm)
        z, xbc, dt = _ssd_proj(h, g_mix, w_z, w_xbc, w_dt, tm)
        gates = _rms_proj(_gate_proj_kernel, h, g_mix, w_gate.astype(BF16), F32, tm, "gate_proj")
        attn = _dsa(q, k, v, iq, ik, iw, batch, seq)
        ssd = _ssd(z, xbc, dt, conv_w[layer], conv_b[layer], dt_bias[layer], a_log[layer], d_skip[layer],
                   ssd_norm_g[layer], batch, seq)
        kv = _rms_proj(_plain_proj_kernel, mem2, row(norm_mem_g[layer]), w_cross_kv[layer].astype(BF16),
                       BF16, mem.shape[1], "mem_kv")
        h2, a3 = _merge(h, attn, ssd, gates, w_attn_branch[layer].astype(BF16), w_ssd_branch[layer].astype(BF16),
                        w_out[layer].astype(BF16), row(norm_cross_g[layer]), w_cross_q[layer].astype(BF16), kv,
                        w_cross_out[layer].astype(BF16), row(norm_ffn_g[layer]), batch, seq, tm)
        sk = peer_sub_keys[layer].reshape(PEER_HEADS * 2, PEER_N_KEYS, PEER_KEY_DIM).astype(BF16)
        tabs = _peer_route(a3, w_peer_q[layer].astype(BF16), sk, 256)
        h = _peer_dense(a3.T, peer_u[layer].astype(BF16), peer_v[layer].T.astype(BF16), tabs, h2,
                        row(norm_final_g), 512, 4, layer == depth - 1)
    return h.reshape(batch, seq, d)
```

```python
import functools
import math

import jax
import jax.numpy as jnp
from jax import lax
from jax.experimental import pallas as pl
from jax.experimental.pallas import tpu as pltpu

F32 = jnp.float32
BF16 = jnp.bfloat16
I32 = jnp.int32

ATTN_HEADS = 8
ATTN_KV_HEADS = 2
ATTN_HEAD_DIM = 128
IDX_HEADS = 4
IDX_HEAD_DIM = 64
TOPK_MAX = 256
ROPE_THETA = 500000.0
ROPE_FRACTION = 4
SSD_HEAD_DIM = 64
SSD_GROUPS = 4
SSD_STATE = 128
SSD_CONV = 4
SSD_CHUNK = 128
MEM_HEADS = 4
PEER_HEADS = 8
PEER_N_KEYS = 128
PEER_KEY_DIM = 128
PEER_TOPK = 16
EPS = 1e-6

LANES = 128
SUBLANES = 8
VMEM_LIMIT = 56 * 1024 * 1024

NEG_BIG = -1e30
INT_MIN = -2 ** 31


def _cparams(*sem):
    return pltpu.CompilerParams(dimension_semantics=sem, vmem_limit_bytes=VMEM_LIMIT)


def _const_spec(shape):
    nd = len(shape)
    return pl.BlockSpec(shape, lambda *_: (0,) * nd, pipeline_mode=pl.Buffered(1))


def _rmsnorm(x, g):
    return x * lax.rsqrt(jnp.mean(x * x, axis=-1, keepdims=True) + EPS) * g


def _split3(a):
    hi = a.astype(BF16)
    r1 = a - hi.astype(F32)
    mid = r1.astype(BF16)
    lo = (r1 - mid.astype(F32)).astype(BF16)
    return hi, mid, lo


def _dot_exact_rhs(a, b_bf16):
    hi, mid, lo = _split3(a)
    d = functools.partial(jnp.dot, preferred_element_type=F32)
    return d(hi, b_bf16) + d(mid, b_bf16) + d(lo, b_bf16)


def _dot_exact_lhs(a_bf16, b):
    hi, mid, lo = _split3(b)
    d = functools.partial(jnp.dot, preferred_element_type=F32)
    return d(a_bf16, hi) + d(a_bf16, mid) + d(a_bf16, lo)


def _dot_nt(a, b):
    return lax.dot_general(a, b, (((1,), (1,)), ((), ())), preferred_element_type=F32)


def _rope(x, cos, sin_signed, first_half, half):
    w = x.shape[-1]
    x_sw = jnp.where(first_half, pltpu.roll(x, w - half, 1), pltpu.roll(x, half, 1))
    return x * cos + x_sw * sin_signed


def _attn_proj_kernel(x_ref, g_ref, w_ref, pos_ref, rope_ref,
                      qt_ref, k_ref, vt_ref, iqt_ref, ik_ref, iwt_ref):
    a = _rmsnorm(x_ref[...], g_ref[...]).astype(BF16)
    p = jnp.dot(a, w_ref[...], preferred_element_type=F32)
    pos = pos_ref[...]
    tabs = []
    for r, half in ((0, ATTN_HEAD_DIM // ROPE_FRACTION // 2), (2, IDX_HEAD_DIM // ROPE_FRACTION // 2)):
        ang = pos * rope_ref[r:r + 1, :]
        sgn = rope_ref[r + 1:r + 2, :]
        tabs.append((jnp.cos(ang), jnp.sin(ang) * sgn, sgn < 0.0, half))
    t128, t64 = tabs
    off = 0
    for h in range(ATTN_HEADS):
        qt_ref[h * LANES:(h + 1) * LANES, :] = jnp.transpose(_rope(p[:, off:off + LANES], *t128)).astype(BF16)
        off += LANES
    for h in range(ATTN_KV_HEADS):
        k_ref[:, h * LANES:(h + 1) * LANES] = _rope(p[:, off:off + LANES], *t128).astype(BF16)
        off += LANES
    for h in range(ATTN_KV_HEADS):
        vt_ref[0, h * LANES:(h + 1) * LANES, :] = jnp.transpose(p[:, off:off + LANES]).astype(BF16)
        off += LANES
    for h in range(IDX_HEADS):
        iqt_ref[h * LANES:(h + 1) * LANES, :] = jnp.transpose(_rope(p[:, off:off + LANES], *t64)).astype(BF16)
        off += LANES
    ik_ref[...] = _rope(p[:, off:off + LANES], *t64).astype(BF16)
    off += LANES
    iwt_ref[...] = jnp.transpose(p[:, off:off + LANES])[0:SUBLANES, :]


def _rope_rows():
    rows = []
    for head_dim in (ATTN_HEAD_DIM, IDX_HEAD_DIM):
        rot = head_dim // ROPE_FRACTION
        half = rot // 2
        inv = ROPE_THETA ** (-2.0 * jnp.arange(half, dtype=F32) / rot)
        inv_row = jnp.zeros((LANES,), F32).at[:rot].set(jnp.concatenate([inv, inv]))
        sgn_row = jnp.zeros((LANES,), F32).at[:half].set(-1.0).at[half:rot].set(1.0)
        rows += [inv_row, sgn_row]
    return jnp.stack(rows)


def _pad_cols(w, width):
    return jnp.pad(w, ((0, 0), (0, width - w.shape[1])))


def _attn_proj(x2, g, w_in, pos_f32):
    n, d = x2.shape
    tm = DSA_KC
    hq = ATTN_HEADS * ATTN_HEAD_DIM
    hkv = ATTN_KV_HEADS * ATTN_HEAD_DIM
    o = 0
    wq = w_in[:, o:o + hq]; o += hq
    wk = w_in[:, o:o + hkv]; o += hkv
    wv = w_in[:, o:o + hkv]; o += hkv
    wiq = w_in[:, o:o + IDX_HEADS * IDX_HEAD_DIM]; o += IDX_HEADS * IDX_HEAD_DIM
    wik = w_in[:, o:o + IDX_HEAD_DIM]; o += IDX_HEAD_DIM
    wiw = w_in[:, o:o + IDX_HEADS]; o += IDX_HEADS
    wiq = jnp.pad(wiq.reshape(d, IDX_HEADS, IDX_HEAD_DIM),
                  ((0, 0), (0, 0), (0, LANES - IDX_HEAD_DIM))).reshape(d, IDX_HEADS * LANES)
    w = jnp.concatenate([wq, wk, wv, wiq, _pad_cols(wik, LANES), _pad_cols(wiw, LANES)], axis=1).astype(BF16)
    nw = w.shape[1]
    row = lambda width: pl.BlockSpec((tm, width), lambda i: (i, 0))
    col = lambda height: pl.BlockSpec((height, tm), lambda i: (0, i))
    outs = pl.pallas_call(
        _attn_proj_kernel,
        grid=(n // tm,),
        in_specs=[row(d), _const_spec((1, d)), _const_spec((d, nw)), row(1), _const_spec((4, LANES))],
        out_specs=[col(hq), row(hkv), pl.BlockSpec((1, hkv, tm), lambda i: (i, 0, 0)),
                   col(IDX_HEADS * LANES), row(LANES), col(SUBLANES)],
        out_shape=[jax.ShapeDtypeStruct((hq, n), BF16), jax.ShapeDtypeStruct((n, hkv), BF16),
                   jax.ShapeDtypeStruct((n // tm, hkv, tm), BF16),
                   jax.ShapeDtypeStruct((IDX_HEADS * LANES, n), BF16),
                   jax.ShapeDtypeStruct((n, LANES), BF16), jax.ShapeDtypeStruct((SUBLANES, n), F32)],
        compiler_params=_cparams("parallel"),
        name="attn_proj",
    )(x2, g, w, pos_f32, _rope_rows())
    return outs


def _ssd_proj_kernel(x_ref, g_ref, w_ref, z_ref, xbc_ref, dt_ref, *, dz, dxbc):
    a = _rmsnorm(x_ref[...], g_ref[...]).astype(BF16)
    z_ref[...] = jnp.dot(a, w_ref[:, :dz], preferred_element_type=F32)
    xbc_ref[...] = jnp.dot(a, w_ref[:, dz:dz + dxbc], preferred_element_type=F32)
    dt_ref[...] = jnp.dot(a, w_ref[:, dz + dxbc:], preferred_element_type=F32)


def _gate_proj_kernel(x_ref, g_ref, w_ref, o_ref):
    a = _rmsnorm(x_ref[...], g_ref[...]).astype(BF16)
    o_ref[...] = jax.nn.sigmoid(jnp.dot(a, w_ref[...], preferred_element_type=F32))


def _plain_proj_kernel(x_ref, g_ref, w_ref, o_ref):
    a = _rmsnorm(x_ref[...], g_ref[...]).astype(BF16)
    o_ref[...] = jnp.dot(a, w_ref[...], preferred_element_type=F32).astype(o_ref.dtype)


def _rms_proj(body, x2, g, w, out_dtype, tm, name):
    n, d = x2.shape
    nw = w.shape[1]
    return pl.pallas_call(
        body,
        grid=(n // tm,),
        in_specs=[pl.BlockSpec((tm, d), lambda i: (i, 0)), _const_spec((1, d)), _const_spec((d, nw))],
        out_specs=pl.BlockSpec((tm, nw), lambda i: (i, 0)),
        out_shape=jax.ShapeDtypeStruct((n, nw), out_dtype),
        compiler_params=_cparams("parallel"),
        name=name,
    )(x2, g, w)


def _ssd_proj(x2, g, w_z, w_xbc, w_dt, tm):
    n, d = x2.shape
    dz, dxbc = w_z.shape[1], w_xbc.shape[1]
    w = jnp.concatenate([w_z, w_xbc, _pad_cols(w_dt, LANES)], axis=1).astype(BF16)
    row = lambda width: pl.BlockSpec((tm, width), lambda i: (i, 0))
    return pl.pallas_call(
        functools.partial(_ssd_proj_kernel, dz=dz, dxbc=dxbc),
        grid=(n // tm,),
        in_specs=[row(d), _const_spec((1, d)), _const_spec((d, w.shape[1]))],
        out_specs=[row(dz), row(dxbc), row(LANES)],
        out_shape=[jax.ShapeDtypeStruct((n, dz), F32), jax.ShapeDtypeStruct((n, dxbc), F32),
                   jax.ShapeDtypeStruct((n, LANES), F32)],
        compiler_params=_cparams("parallel"),
        name="ssd_proj",
    )(x2, g, w)


DSA_QB = 128
DSA_KC = 512
COUNT_CHAINS = 8


def _dsa_kernel(iqt_ref, iwt_ref, qt_ref, ik_ref, k_ref, vt_ref, tri_ref, o_ref,
                key_ref, m_ref, l_ref, acc_ref, *, nsel):
    qb = pl.program_id(1)
    nchunk = qb // (DSA_KC // DSA_QB) + 1
    grp = ATTN_HEADS // ATTN_KV_HEADS
    idx_scale = IDX_HEAD_DIM ** -0.5
    w_scale = IDX_HEADS ** -0.5
    exp2_scale = ATTN_HEAD_DIM ** -0.5 * math.log2(math.e)

    t_glob = qb * DSA_QB + lax.broadcasted_iota(I32, (DSA_KC, DSA_QB), 1)
    s_loc = lax.broadcasted_iota(I32, (DSA_KC, DSA_QB), 0)
    t_glob_b = qb * DSA_QB + lax.broadcasted_iota(I32, (DSA_QB, DSA_QB), 1)
    s_loc_b = lax.broadcasted_iota(I32, (DSA_QB, DSA_QB), 0)
    iw_t = iwt_ref[...] * w_scale
    iq_all = jnp.concatenate([iqt_ref[h * LANES:(h + 1) * LANES, :] for h in range(IDX_HEADS)], axis=1)
    q_all = [jnp.concatenate([qt_ref[(kvh * grp + g) * LANES:(kvh * grp + g + 1) * LANES, :]
                              for g in range(grp)], axis=1) for kvh in range(ATTN_KV_HEADS)]

    def score_chunk(c, carry):
        ikc = ik_ref[pl.ds(c * DSA_KC, DSA_KC), :]
        d = jnp.dot(ikc, iq_all, preferred_element_type=F32)
        sc = jnp.zeros((DSA_KC, DSA_QB), F32)
        for h in range(IDX_HEADS):
            sc = sc + jnp.maximum(d[:, h * DSA_QB:(h + 1) * DSA_QB] * idx_scale, 0.0) * iw_t[h:h + 1, :]
        sc = jnp.where(c * DSA_KC + s_loc <= t_glob, sc, -jnp.inf)
        bits = pltpu.bitcast(sc, I32)
        key_ref[pl.ds(c * DSA_KC, DSA_KC), :] = jnp.where(bits < 0, bits ^ 0x7FFFFFFF, bits)
        return carry

    lax.fori_loop(0, nchunk, score_chunk, 0)

    def count_ge(cand):
        def body(c, accs):
            accs = list(accs)
            for j in range(DSA_KC // SUBLANES):
                blk = key_ref[pl.ds(c * DSA_KC + j * SUBLANES, SUBLANES), :]
                accs[j % COUNT_CHAINS] = accs[j % COUNT_CHAINS] + jnp.where(blk >= cand, 1, 0)
            return tuple(accs)
        zero = jnp.zeros((SUBLANES, DSA_QB), I32)
        accs = lax.fori_loop(0, nchunk, body, (zero,) * COUNT_CHAINS)
        return jnp.sum(functools.reduce(lambda x, y: x + y, accs), axis=0, keepdims=True)

    def bisect(it, ans):
        cand = ans + lax.shift_left(jnp.int32(1), 31 - it)
        cnt = count_ge(cand)
        return jnp.where(cnt >= nsel, cand, ans)

    tau = lax.fori_loop(0, 32, bisect, jnp.full((SUBLANES, DSA_QB), INT_MIN, I32))
    n_gt = count_ge(tau + 1)
    n_tie = (nsel - n_gt).astype(F32)
    tau_row = tau[0:1, :]

    m_ref[...] = jnp.full(m_ref.shape, NEG_BIG, F32)
    l_ref[...] = jnp.zeros(l_ref.shape, F32)
    acc_ref[...] = jnp.zeros(acc_ref.shape, F32)

    def attn_chunk(c, tie_carry):
        biases = []
        for sb in range(DSA_KC // DSA_QB):
            keys = key_ref[pl.ds(c * DSA_KC + sb * DSA_QB, DSA_QB), :]
            eq = keys == tau_row
            prefix = jnp.dot(tri_ref[...], jnp.where(eq, 1.0, 0.0).astype(BF16),
                             preferred_element_type=F32) + tie_carry
            tie_carry = prefix[DSA_QB - 1:DSA_QB, :]
            sel = (keys > tau_row) | (eq & (prefix <= n_tie))
            sel = sel & (c * DSA_KC + sb * DSA_QB + s_loc_b <= t_glob_b)
            biases.append(jnp.where(sel, 0.0, NEG_BIG))
        bias = jnp.concatenate(biases, axis=0)
        bias = jnp.concatenate([bias] * grp, axis=1)
        for kvh in range(ATTN_KV_HEADS):
            kc = k_ref[pl.ds(c * DSA_KC, DSA_KC), kvh * LANES:(kvh + 1) * LANES]
            vtc = vt_ref[c, kvh * LANES:(kvh + 1) * LANES, :]
            s = jnp.dot(kc, q_all[kvh], preferred_element_type=F32) + bias
            m_old = m_ref[kvh]
            m_new = jnp.maximum(m_old, jnp.max(s, axis=0, keepdims=True))
            p = jnp.exp2((s - m_new) * exp2_scale)
            alpha = jnp.exp2((m_old - m_new) * exp2_scale)
            l_ref[kvh] = alpha * l_ref[kvh] + jnp.sum(p, axis=0, keepdims=True)
            acc_ref[kvh] = alpha * acc_ref[kvh] + jnp.dot(vtc, p.astype(BF16), preferred_element_type=F32)
            m_ref[kvh] = m_new
        return tie_carry

    lax.fori_loop(0, nchunk, attn_chunk, jnp.zeros((1, DSA_QB), F32))
    for kvh in range(ATTN_KV_HEADS):
        o_t = acc_ref[kvh] / l_ref[kvh]
        for g in range(grp):
            h = kvh * grp + g
            o_ref[:, h * LANES:(h + 1) * LANES] = jnp.transpose(
                o_t[:, g * DSA_QB:(g + 1) * DSA_QB]).astype(o_ref.dtype)


def _dsa(qt, k, vt, iqt, ik, iwt, batch, seq):
    n = batch * seq
    nsel = min(TOPK_MAX, seq // 4)
    nqb = seq // DSA_QB
    grp_lanes = ATTN_HEADS // ATTN_KV_HEADS * DSA_QB
    hq = ATTN_HEADS * ATTN_HEAD_DIM
    hkv = ATTN_KV_HEADS * ATTN_HEAD_DIM
    tri = (jnp.arange(DSA_QB)[:, None] >= jnp.arange(DSA_QB)[None, :]).astype(BF16)
    qcol = lambda height: pl.BlockSpec((height, DSA_QB), lambda b, i: (0, b * nqb + i))
    full = lambda width: pl.BlockSpec((seq, width), lambda b, i: (b, 0))
    return pl.pallas_call(
        functools.partial(_dsa_kernel, nsel=nsel),
        grid=(batch, nqb),
        in_specs=[qcol(IDX_HEADS * LANES), qcol(SUBLANES), qcol(hq), full(LANES), full(hkv),
                  pl.BlockSpec((seq // DSA_KC, hkv, DSA_KC), lambda b, i: (b, 0, 0)),
                  _const_spec((DSA_QB, DSA_QB))],
        out_specs=pl.BlockSpec((DSA_QB, hq), lambda b, i: (b * nqb + i, 0)),
        out_shape=jax.ShapeDtypeStruct((n, hq), BF16),
        scratch_shapes=[pltpu.VMEM((seq, DSA_QB), I32),
                        pltpu.VMEM((ATTN_KV_HEADS, 1, grp_lanes), F32),
                        pltpu.VMEM((ATTN_KV_HEADS, 1, grp_lanes), F32),
                        pltpu.VMEM((ATTN_KV_HEADS, ATTN_HEAD_DIM, grp_lanes), F32)],
        compiler_params=_cparams("arbitrary", "arbitrary"),
        name="dsa",
    )(iqt, iwt, qt, ik, k, vt, tri)


CONV_HALO = 8


def _ssd_kernel(xbc_ref, z_ref, dt_ref, cw_ref, cb_ref, dtb_ref, alog_ref, dskip_ref, ng_ref,
                tri_ref, exp_ref, o_ref, ext_ref, state_ref, *, d_inner):
    L = SSD_CHUNK
    gn = SSD_GROUPS * SSD_STATE
    heads_per_group = d_inner // SSD_HEAD_DIM // SSD_GROUPS
    gw = heads_per_group * SSD_HEAD_DIM

    @pl.when(pl.program_id(1) == 0)
    def _():
        ext_ref[0:CONV_HALO, :] = jnp.zeros((CONV_HALO, ext_ref.shape[1]), F32)
        state_ref[...] = jnp.zeros(state_ref.shape, F32)

    raw = xbc_ref[...]
    ext_ref[CONV_HALO:CONV_HALO + L, :] = raw
    acc = cb_ref[...] + jnp.zeros_like(raw)
    for kk in range(SSD_CONV):
        start = CONV_HALO - (SSD_CONV - 1) + kk
        acc = acc + cw_ref[kk:kk + 1, :] * ext_ref[start:start + L, :]
    ext_ref[0:CONV_HALO, :] = raw[L - CONV_HALO:L, :]
    xbc = acc * jax.nn.sigmoid(acc)
    xs = xbc[:, :d_inner]
    bm = xbc[:, d_inner:d_inner + gn]
    cm = xbc[:, d_inner + gn:]

    dt = jax.nn.softplus(dt_ref[...] + dtb_ref[...])
    a = -jnp.exp(alog_ref[...])
    acs = _dot_exact_lhs(tri_ref[...], dt * a)
    acs_t = jnp.transpose(acs)
    e_acs = jnp.exp(acs)
    e_end = jnp.exp(acs[L - 1:L, :] - acs)
    expand = exp_ref[...]
    dt_x = _dot_exact_rhs(dt, expand)
    e_acs_x = _dot_exact_rhs(e_acs, expand)
    e_end_x = _dot_exact_rhs(e_end, expand)
    xdt = xs * dt_x
    xdt_b = xdt.astype(BF16)
    xw_b = (xdt * e_end_x).astype(BF16)

    lower = lax.broadcasted_iota(I32, (L, L), 0) >= lax.broadcasted_iota(I32, (L, L), 1)
    lane = lax.broadcasted_iota(I32, (L, LANES), 1)
    heads_per_tile = LANES // SSD_HEAD_DIM
    for g in range(SSD_GROUPS):
        cg = cm[:, g * SSD_STATE:(g + 1) * SSD_STATE].astype(BF16)
        bg = bm[:, g * SSD_STATE:(g + 1) * SSD_STATE]
        cb = _dot_nt(cg, bg.astype(BF16))
        st = state_ref[:, g * gw:(g + 1) * gw]
        y_off = jnp.dot(cg, st.astype(BF16), preferred_element_type=F32) * e_acs_x[:, g * gw:(g + 1) * gw]
        for tile in range(gw // LANES):
            col = g * gw + tile * LANES
            x_tile = xdt_b[:, col:col + LANES]
            y_tile = y_off[:, tile * LANES:(tile + 1) * LANES]
            for sub in range(heads_per_tile):
                h = col // SSD_HEAD_DIM + sub
                seg = acs[:, h:h + 1] - acs_t[h:h + 1, :]
                m = (cb * jnp.exp(jnp.where(lower, seg, -jnp.inf))).astype(BF16)
                in_head = (lane >= sub * SSD_HEAD_DIM) & (lane < (sub + 1) * SSD_HEAD_DIM)
                y_tile = y_tile + jnp.dot(m, jnp.where(in_head, x_tile, jnp.zeros_like(x_tile)),
                                          preferred_element_type=F32)
            y_tile = y_tile + xs[:, col:col + LANES] * dskip_ref[:, col:col + LANES]
            zt = z_ref[:, col:col + LANES]
            o_ref[:, col:col + LANES] = (y_tile * (zt * jax.nn.sigmoid(zt))).astype(o_ref.dtype)
        bg_t = jnp.transpose(bg).astype(BF16)
        state_ref[:, g * gw:(g + 1) * gw] = (
            st * e_acs_x[L - 1:L, g * gw:(g + 1) * gw]
            + jnp.dot(bg_t, xw_b[:, g * gw:(g + 1) * gw], preferred_element_type=F32))

    y = o_ref[...].astype(F32)
    o_ref[...] = _rmsnorm(y, ng_ref[...]).astype(o_ref.dtype)


def _ssd(z, xbc, dt, conv_w, conv_b, dt_bias, a_log, d_skip, norm_g, batch, seq):
    n, d_inner = z.shape
    cch = xbc.shape[1]
    heads = d_inner // SSD_HEAD_DIM
    nc = seq // SSD_CHUNK
    L = SSD_CHUNK
    tri = (jnp.arange(L)[:, None] >= jnp.arange(L)[None, :]).astype(BF16)
    expand = (jnp.arange(LANES)[:, None] == (jnp.arange(d_inner)[None, :] // SSD_HEAD_DIM)).astype(BF16)
    pad_h = lambda v: jnp.pad(v.reshape(1, heads), ((0, 0), (0, LANES - heads)))
    row = lambda width: pl.BlockSpec((L, width), lambda b, c: (b * nc + c, 0))
    return pl.pallas_call(
        functools.partial(_ssd_kernel, d_inner=d_inner),
        grid=(batch, nc),
        in_specs=[row(cch), row(d_inner), row(LANES),
                  _const_spec((SSD_CONV, cch)), _const_spec((1, cch)), _const_spec((1, LANES)),
                  _const_spec((1, LANES)), _const_spec((1, d_inner)), _const_spec((1, d_inner)),
                  _const_spec((L, L)), _const_spec((LANES, d_inner))],
        out_specs=row(d_inner),
        out_shape=jax.ShapeDtypeStruct((n, d_inner), F32),
        scratch_shapes=[pltpu.VMEM((CONV_HALO + L, cch), F32), pltpu.VMEM((SSD_STATE, d_inner), F32)],
        compiler_params=_cparams("arbitrary", "arbitrary"),
        name="ssd",
    )(xbc, z, dt, conv_w.reshape(SSD_CONV, cch), conv_b.reshape(1, cch), pad_h(dt_bias), pad_h(a_log),
      jnp.repeat(d_skip, SSD_HEAD_DIM).reshape(1, d_inner), norm_g.reshape(1, d_inner), tri, expand)


def _merge_kernel(x_ref, attn_ref, ssd_ref, gate_ref, wab_ref, wsb_ref, wout_ref, gc_ref, wcq_ref,
                  kv_ref, wco_ref, gf_ref, h_ref, a_ref, *, d_model):
    dot = functools.partial(jnp.dot, preferred_element_type=F32)
    br_a = dot(attn_ref[...], wab_ref[...])
    br_s = dot(ssd_ref[...].astype(BF16), wsb_ref[...])
    merged = gate_ref[:, :d_model] * br_a + gate_ref[:, d_model:] * br_s
    h = x_ref[...] + dot(merged.astype(BF16), wout_ref[...])

    qc = dot(_rmsnorm(h, gc_ref[...]).astype(BF16), wcq_ref[...]).astype(BF16)
    dh = d_model // MEM_HEADS
    outs = []
    for hd in range(MEM_HEADS):
        kk = kv_ref[:, hd * dh:(hd + 1) * dh]
        vv = kv_ref[:, d_model + hd * dh:d_model + (hd + 1) * dh]
        logits = _dot_nt(qc[:, hd * dh:(hd + 1) * dh], kk) * dh ** -0.5
        e = jnp.exp(logits - jnp.max(logits, axis=-1, keepdims=True))
        p = e / jnp.sum(e, axis=-1, keepdims=True)
        outs.append(dot(p.astype(BF16), vv))
    o = jnp.concatenate(outs, axis=-1)
    h = h + dot(o.astype(BF16), wco_ref[...])
    h_ref[...] = h
    a_ref[...] = _rmsnorm(h, gf_ref[...]).astype(a_ref.dtype)


def _merge(x2, attn, ssd, gates, w_ab, w_sb, w_out, g_cross, w_cq, kv, w_co, g_ffn, batch, seq, tm):
    n, d = x2.shape
    mem_len = kv.shape[0] // batch
    per_b = seq // tm
    row = lambda width: pl.BlockSpec((tm, width), lambda i: (i, 0))
    return pl.pallas_call(
        functools.partial(_merge_kernel, d_model=d),
        grid=(n // tm,),
        in_specs=[row(d), row(attn.shape[1]), row(ssd.shape[1]), row(gates.shape[1]),
                  _const_spec(w_ab.shape), _const_spec(w_sb.shape), _const_spec(w_out.shape),
                  _const_spec((1, d)), _const_spec(w_cq.shape),
                  pl.BlockSpec((mem_len, kv.shape[1]), lambda i: (i // per_b, 0)),
                  _const_spec(w_co.shape), _const_spec((1, d))],
        out_specs=[row(d), row(d)],
        out_shape=[jax.ShapeDtypeStruct((n, d), F32), jax.ShapeDtypeStruct((n, d), BF16)],
        compiler_params=_cparams("parallel"),
        name="merge",
    )(x2, attn, ssd, gates, w_ab, w_sb, w_out, g_cross, w_cq, kv, w_co, g_ffn)


def _top16_rows(s_t):
    nk, tn = s_t.shape
    key_iota = lax.broadcasted_iota(I32, (nk, tn), 0).astype(F32)
    slot_iota = lax.broadcasted_iota(I32, (PEER_TOPK, tn), 0)
    cur = s_t
    rank = jnp.full((nk, tn), float(PEER_TOPK), F32)
    vals = jnp.zeros((PEER_TOPK, tn), F32)
    for r in range(PEER_TOPK):
        m = jnp.max(cur, axis=0, keepdims=True)
        first = jnp.min(jnp.where(cur == m, key_iota, float(nk)), axis=0, keepdims=True)
        hit = key_iota == first
        rank = jnp.where(hit, float(r), rank)
        cur = jnp.where(hit, -jnp.inf, cur)
        vals = jnp.where(slot_iota == r, m, vals)
    return vals, rank


def _pair_merge(v1, v2):
    k, tn = v1.shape
    slot = lax.broadcasted_iota(I32, (k, tn), 0).astype(F32)
    slot_f = slot
    count = jnp.zeros((k, tn), F32)
    front = v1 + v2[0:1, :]
    best = front[0:1, :]
    z = jnp.zeros((1, tn), F32)
    for _ in range(PEER_TOPK):
        m = jnp.max(front, axis=0, keepdims=True)
        first = jnp.min(jnp.where(front == m, slot, float(k)), axis=0, keepdims=True)
        hit = slot == first
        z = z + jnp.exp(m - best)
        count = jnp.where(hit, count + 1.0, count)
        nxt = jnp.sum(jnp.where(hit, count, 0.0), axis=0, keepdims=True)
        v2n = jnp.sum(jnp.where(slot_f == nxt, v2, 0.0), axis=0, keepdims=True)
        v2n = jnp.where(nxt >= float(k), -jnp.inf, v2n)
        front = jnp.where(hit, v1 + v2n, front)
    return count, z


def _peer_route_kernel(a_ref, wq_ref, sk_ref, r2_ref, g2_ref, la_ref, c1_ref):
    qry = jnp.dot(a_ref[...], wq_ref[...], preferred_element_type=F32).astype(BF16)
    for h in range(PEER_HEADS):
        halves = []
        for p in range(2):
            hp = h * 2 + p
            s_t = _dot_nt(sk_ref[hp], qry[:, hp * PEER_KEY_DIM:(hp + 1) * PEER_KEY_DIM])
            vals, rank = _top16_rows(s_t)
            halves.append((s_t, vals, rank))
        (s1, v1, rank1), (s2, v2, rank2) = halves
        count, z = _pair_merge(v1, v2)
        la = jnp.zeros_like(rank1)
        for i in range(PEER_TOPK):
            la = la + jnp.where(rank1 == float(i), count[i:i + 1, :], 0.0)
        r2_ref[h] = rank2.astype(r2_ref.dtype)
        g2_ref[h] = (jnp.exp(s2 - v2[0:1, :]) / z).astype(g2_ref.dtype)
        la_ref[h] = la
        c1_ref[h] = jnp.exp(s1 - v1[0:1, :])


def _peer_route(a3, w_pq, sub_keys, tn):
    n, d = a3.shape
    nk = sub_keys.shape[1]
    tab = lambda dt: jax.ShapeDtypeStruct((PEER_HEADS, nk, n), dt)
    tspec = pl.BlockSpec((PEER_HEADS, nk, tn), lambda i: (0, 0, i))
    return pl.pallas_call(
        _peer_route_kernel,
        grid=(n // tn,),
        in_specs=[pl.BlockSpec((tn, d), lambda i: (i, 0)), _const_spec(w_pq.shape), _const_spec(sub_keys.shape)],
        out_specs=[tspec] * 4,
        out_shape=[tab(BF16), tab(BF16), tab(F32), tab(F32)],
        compiler_params=_cparams("parallel"),
        name="peer_route",
    )(a3, w_pq, sub_keys)


PEER_SUB = 2


def _gelu_exact(x):
    return 0.5 * x * (1.0 + lax.erf(x * (2.0 ** -0.5)))


def _peer_dense_kernel(at_ref, u_ref, vt_ref, r2_ref, g2_ref, la_ref, c1_ref, h_ref, gfin_ref,
                       o_ref, yt_ref, *, na, final_norm):
    j = pl.program_id(1)

    @pl.when(j == 0)
    def _():
        yt_ref[...] = jnp.zeros(yt_ref.shape, F32)

    nk = PEER_N_KEYS
    tn = at_ref.shape[1]
    y = None
    nsub = na // PEER_SUB
    rows_of = lambda sub: slice(sub * PEER_SUB * nk, (sub + 1) * PEER_SUB * nk)
    z_of = lambda sub: jnp.dot(u_ref[rows_of(sub), :], at_ref[...], preferred_element_type=F32)
    z_next = z_of(0)
    for sub in range(nsub):
        rows = rows_of(sub)
        z_t = z_next
        if sub + 1 < nsub:
            z_next = z_of(sub + 1)
        parts = []
        for ai in range(PEER_SUB):
            a_idx = j * na + sub * PEER_SUB + ai
            w = jnp.zeros((nk, tn), BF16)
            for h in range(PEER_HEADS):
                la = la_ref[h, pl.ds(a_idx, 1), :].astype(BF16)
                c1 = c1_ref[h, pl.ds(a_idx, 1), :].astype(BF16)
                w = w + jnp.where(r2_ref[h] < la, g2_ref[h], jnp.zeros((), BF16)) * c1
            parts.append(w * _gelu_exact(z_t[ai * nk:(ai + 1) * nk, :]).astype(BF16))
        y_sub = jnp.dot(vt_ref[:, rows], jnp.concatenate(parts, axis=0), preferred_element_type=F32)
        y = y_sub if y is None else y + y_sub
    yt_ref[...] += y

    @pl.when(j == pl.num_programs(1) - 1)
    def _():
        h = h_ref[...] + jnp.transpose(yt_ref[...])
        o_ref[...] = _rmsnorm(h, gfin_ref[...]) if final_norm else h


def _peer_dense(a3_t, u_bf, v_t, tabs, h2, g_final, tn, na, final_norm):
    d, n = a3_t.shape
    n_exp = u_bf.shape[0]
    nk = PEER_N_KEYS
    tspec = pl.BlockSpec((PEER_HEADS, nk, tn), lambda i, j: (0, 0, i))
    return pl.pallas_call(
        functools.partial(_peer_dense_kernel, na=na, final_norm=final_norm),
        grid=(n // tn, n_exp // (na * nk)),
        in_specs=[pl.BlockSpec((d, tn), lambda i, j: (0, i)),
                  pl.BlockSpec((na * nk, d), lambda i, j: (j, 0)),
                  pl.BlockSpec((d, na * nk), lambda i, j: (0, j)),
                  tspec, tspec, tspec, tspec,
                  pl.BlockSpec((tn, d), lambda i, j: (i, 0)),
                  _const_spec((1, d))],
        out_specs=pl.BlockSpec((tn, d), lambda i, j: (i, 0)),
        out_shape=jax.ShapeDtypeStruct((n, d), F32),
        scratch_shapes=[pltpu.VMEM((d, tn), F32)],
        compiler_params=_cparams("parallel", "arbitrary"),
        name="peer_dense",
    )(a3_t, u_bf, v_t, *tabs, h2, g_final)


def _split_w_in(w_in, d_model, d_inner, conv_ch, ssd_heads):
    sizes = (ATTN_HEADS * ATTN_HEAD_DIM, ATTN_KV_HEADS * ATTN_HEAD_DIM, ATTN_KV_HEADS * ATTN_HEAD_DIM,
             IDX_HEADS * IDX_HEAD_DIM, IDX_HEAD_DIM, IDX_HEADS, d_inner, conv_ch, ssd_heads, 2 * d_model)
    assert sum(sizes) == w_in.shape[1]
    attn_w = sum(sizes[:6])
    o = attn_w
    w_z = w_in[:, o:o + d_inner]; o += d_inner
    w_xbc = w_in[:, o:o + conv_ch]; o += conv_ch
    w_dt = w_in[:, o:o + ssd_heads]; o += ssd_heads
    w_gate = w_in[:, o:]
    return w_in[:, :attn_w], w_z, w_xbc, w_dt, w_gate


def kernel(x, mem, positions, norm_mix_g, w_in, conv_w, conv_b, dt_bias, a_log, d_skip, ssd_norm_g, w_attn_branch, w_ssd_branch, w_out, norm_cross_g, norm_mem_g, w_cross_q, w_cross_kv, w_cross_out, norm_ffn_g, w_peer_q, peer_sub_keys, peer_u, peer_v, norm_final_g):
    batch, seq, d = x.shape
    n = batch * seq
    depth = w_in.shape[0]
    d_inner = ssd_norm_g.shape[1]
    conv_ch = conv_b.shape[1]
    ssd_heads = dt_bias.shape[1]
    tm = 256
    row = lambda v: v.reshape(1, -1)

    h = x.reshape(n, d)
    pos = positions.reshape(n, 1).astype(F32)
    mem2 = mem.reshape(batch * mem.shape[1], d)
    a3 = tabs = None
    for layer in range(depth):
        w_attn, w_z, w_xbc, w_dt, w_gate = _split_w_in(w_in[layer], d, d_inner, conv_ch, ssd_heads)
        g_mix = row(norm_mix_g[layer])
        qt, k, vt, iqt, ik, iwt = _attn_proj(h, g_mix, w_attn, pos)
        z, xbc, dt = _ssd_proj(h, g_mix, w_z, w_xbc, w_dt, tm)
        gates = _rms_proj(_gate_proj_kernel, h, g_mix, w_gate.astype(BF16), F32, tm, "gate_proj")
        attn = _dsa(qt, k, vt, iqt, ik, iwt, batch, seq)
        ssd = _ssd(z, xbc, dt, conv_w[layer], conv_b[layer], dt_bias[layer], a_log[layer], d_skip[layer],
                   ssd_norm_g[layer], batch, seq)
        kv = _rms_proj(_plain_proj_kernel, mem2, row(norm_mem_g[layer]), w_cross_kv[layer].astype(BF16),
                       BF16, mem.shape[1], "mem_kv")
        h2, a3 = _merge(h, attn, ssd, gates, w_attn_branch[layer].astype(BF16), w_ssd_branch[layer].astype(BF16),
                        w_out[layer].astype(BF16), row(norm_cross_g[layer]), w_cross_q[layer].astype(BF16), kv,
                        w_cross_out[layer].astype(BF16), row(norm_ffn_g[layer]), batch, seq, tm)
        sk = peer_sub_keys[layer].reshape(PEER_HEADS * 2, PEER_N_KEYS, PEER_KEY_DIM).astype(BF16)
        tabs = _peer_route(a3, w_peer_q[layer].astype(BF16), sk, 256)
        h = _peer_dense(a3.T, peer_u[layer].astype(BF16), peer_v[layer].T.astype(BF16), tabs, h2,
                        row(norm_final_g), 512, 8, layer == depth - 1)
    return h.reshape(batch, seq, d)
```

```python
import functools
import math

import jax
import jax.numpy as jnp
from jax import lax
from jax.experimental import pallas as pl
from jax.experimental.pallas import tpu as pltpu

F32 = jnp.float32
BF16 = jnp.bfloat16
I32 = jnp.int32

ATTN_HEADS = 8
ATTN_KV_HEADS = 2
ATTN_HEAD_DIM = 128
IDX_HEADS = 4
IDX_HEAD_DIM = 64
TOPK_MAX = 256
ROPE_THETA = 500000.0
ROPE_FRACTION = 4
SSD_HEAD_DIM = 64
SSD_GROUPS = 4
SSD_STATE = 128
SSD_CONV = 4
SSD_CHUNK = 128
MEM_HEADS = 4
PEER_HEADS = 8
PEER_N_KEYS = 128
PEER_KEY_DIM = 128
PEER_TOPK = 16
EPS = 1e-6

LANES = 128
SUBLANES = 8
VMEM_LIMIT = 56 * 1024 * 1024

NEG_BIG = -1e30
INT_MIN = -2 ** 31


def _cparams(*sem):
    return pltpu.CompilerParams(dimension_semantics=sem, vmem_limit_bytes=VMEM_LIMIT)


def _const_spec(shape):
    nd = len(shape)
    return pl.BlockSpec(shape, lambda *_: (0,) * nd, pipeline_mode=pl.Buffered(1))


def _rmsnorm(x, g):
    return x * lax.rsqrt(jnp.mean(x * x, axis=-1, keepdims=True) + EPS) * g


def _split3(a):
    hi = a.astype(BF16)
    r1 = a - hi.astype(F32)
    mid = r1.astype(BF16)
    lo = (r1 - mid.astype(F32)).astype(BF16)
    return hi, mid, lo


def _dot_exact_rhs(a, b_bf16):
    hi, mid, lo = _split3(a)
    d = functools.partial(jnp.dot, preferred_element_type=F32)
    return d(hi, b_bf16) + d(mid, b_bf16) + d(lo, b_bf16)


def _dot_exact_lhs(a_bf16, b):
    hi, mid, lo = _split3(b)
    d = functools.partial(jnp.dot, preferred_element_type=F32)
    return d(a_bf16, hi) + d(a_bf16, mid) + d(a_bf16, lo)


def _dot_nt(a, b):
    return lax.dot_general(a, b, (((1,), (1,)), ((), ())), preferred_element_type=F32)


def _rope(x, cos, sin_signed, first_half, half):
    w = x.shape[-1]
    x_sw = jnp.where(first_half, pltpu.roll(x, w - half, 1), pltpu.roll(x, half, 1))
    return x * cos + x_sw * sin_signed


def _attn_proj_kernel(x_ref, g_ref, w_ref, pos_ref, rope_ref,
                      qt_ref, k_ref, vt_ref, iqt_ref, ik_ref, iwt_ref):
    a = _rmsnorm(x_ref[...], g_ref[...]).astype(BF16)
    p = jnp.dot(a, w_ref[...], preferred_element_type=F32)
    pos = pos_ref[...]
    tabs = []
    for r, half in ((0, ATTN_HEAD_DIM // ROPE_FRACTION // 2), (2, IDX_HEAD_DIM // ROPE_FRACTION // 2)):
        ang = pos * rope_ref[r:r + 1, :]
        sgn = rope_ref[r + 1:r + 2, :]
        tabs.append((jnp.cos(ang), jnp.sin(ang) * sgn, sgn < 0.0, half))
    t128, t64 = tabs
    off = 0
    for h in range(ATTN_HEADS):
        qt_ref[h * LANES:(h + 1) * LANES, :] = jnp.transpose(_rope(p[:, off:off + LANES], *t128)).astype(BF16)
        off += LANES
    for h in range(ATTN_KV_HEADS):
        k_ref[:, h * LANES:(h + 1) * LANES] = _rope(p[:, off:off + LANES], *t128).astype(BF16)
        off += LANES
    for h in range(ATTN_KV_HEADS):
        vt_ref[0, h * LANES:(h + 1) * LANES, :] = jnp.transpose(p[:, off:off + LANES]).astype(BF16)
        off += LANES
    for h in range(IDX_HEADS):
        iqt_ref[h * LANES:(h + 1) * LANES, :] = jnp.transpose(_rope(p[:, off:off + LANES], *t64)).astype(BF16)
        off += LANES
    ik_ref[...] = _rope(p[:, off:off + LANES], *t64).astype(BF16)
    off += LANES
    iwt_ref[...] = jnp.transpose(p[:, off:off + LANES])[0:SUBLANES, :]


def _rope_rows():
    rows = []
    for head_dim in (ATTN_HEAD_DIM, IDX_HEAD_DIM):
        rot = head_dim // ROPE_FRACTION
        half = rot // 2
        inv = ROPE_THETA ** (-2.0 * jnp.arange(half, dtype=F32) / rot)
        inv_row = jnp.zeros((LANES,), F32).at[:rot].set(jnp.concatenate([inv, inv]))
        sgn_row = jnp.zeros((LANES,), F32).at[:half].set(-1.0).at[half:rot].set(1.0)
        rows += [inv_row, sgn_row]
    return jnp.stack(rows)


def _pad_cols(w, width):
    return jnp.pad(w, ((0, 0), (0, width - w.shape[1])))


def _attn_proj(x2, g, w_in, pos_f32):
    n, d = x2.shape
    tm = DSA_KC
    hq = ATTN_HEADS * ATTN_HEAD_DIM
    hkv = ATTN_KV_HEADS * ATTN_HEAD_DIM
    o = 0
    wq = w_in[:, o:o + hq]; o += hq
    wk = w_in[:, o:o + hkv]; o += hkv
    wv = w_in[:, o:o + hkv]; o += hkv
    wiq = w_in[:, o:o + IDX_HEADS * IDX_HEAD_DIM]; o += IDX_HEADS * IDX_HEAD_DIM
    wik = w_in[:, o:o + IDX_HEAD_DIM]; o += IDX_HEAD_DIM
    wiw = w_in[:, o:o + IDX_HEADS]; o += IDX_HEADS
    wiq = jnp.pad(wiq.reshape(d, IDX_HEADS, IDX_HEAD_DIM),
                  ((0, 0), (0, 0), (0, LANES - IDX_HEAD_DIM))).reshape(d, IDX_HEADS * LANES)
    w = jnp.concatenate([wq, wk, wv, wiq, _pad_cols(wik, LANES), _pad_cols(wiw, LANES)], axis=1).astype(BF16)
    nw = w.shape[1]
    row = lambda width: pl.BlockSpec((tm, width), lambda i: (i, 0))
    col = lambda height: pl.BlockSpec((height, tm), lambda i: (0, i))
    outs = pl.pallas_call(
        _attn_proj_kernel,
        grid=(n // tm,),
        in_specs=[row(d), _const_spec((1, d)), _const_spec((d, nw)), row(1), _const_spec((4, LANES))],
        out_specs=[col(hq), row(hkv), pl.BlockSpec((1, hkv, tm), lambda i: (i, 0, 0)),
                   col(IDX_HEADS * LANES), row(LANES), col(SUBLANES)],
        out_shape=[jax.ShapeDtypeStruct((hq, n), BF16), jax.ShapeDtypeStruct((n, hkv), BF16),
                   jax.ShapeDtypeStruct((n // tm, hkv, tm), BF16),
                   jax.ShapeDtypeStruct((IDX_HEADS * LANES, n), BF16),
                   jax.ShapeDtypeStruct((n, LANES), BF16), jax.ShapeDtypeStruct((SUBLANES, n), F32)],
        compiler_params=_cparams("parallel"),
        name="attn_proj",
    )(x2, g, w, pos_f32, _rope_rows())
    return outs


def _ssd_proj_kernel(x_ref, g_ref, w_ref, z_ref, xbc_ref, dt_ref, *, dz, dxbc):
    a = _rmsnorm(x_ref[...], g_ref[...]).astype(BF16)
    z_ref[...] = jnp.dot(a, w_ref[:, :dz], preferred_element_type=F32)
    xbc_ref[...] = jnp.dot(a, w_ref[:, dz:dz + dxbc], preferred_element_type=F32)
    dt_ref[...] = jnp.dot(a, w_ref[:, dz + dxbc:], preferred_element_type=F32)


def _gate_proj_kernel(x_ref, g_ref, w_ref, o_ref):
    a = _rmsnorm(x_ref[...], g_ref[...]).astype(BF16)
    o_ref[...] = jax.nn.sigmoid(jnp.dot(a, w_ref[...], preferred_element_type=F32))


def _plain_proj_kernel(x_ref, g_ref, w_ref, o_ref):
    a = _rmsnorm(x_ref[...], g_ref[...]).astype(BF16)
    o_ref[...] = jnp.dot(a, w_ref[...], preferred_element_type=F32).astype(o_ref.dtype)


def _rms_proj(body, x2, g, w, out_dtype, tm, name):
    n, d = x2.shape
    nw = w.shape[1]
    return pl.pallas_call(
        body,
        grid=(n // tm,),
        in_specs=[pl.BlockSpec((tm, d), lambda i: (i, 0)), _const_spec((1, d)), _const_spec((d, nw))],
        out_specs=pl.BlockSpec((tm, nw), lambda i: (i, 0)),
        out_shape=jax.ShapeDtypeStruct((n, nw), out_dtype),
        compiler_params=_cparams("parallel"),
        name=name,
    )(x2, g, w)


def _ssd_proj(x2, g, w_z, w_xbc, w_dt, tm):
    n, d = x2.shape
    dz, dxbc = w_z.shape[1], w_xbc.shape[1]
    w = jnp.concatenate([w_z, w_xbc, _pad_cols(w_dt, LANES)], axis=1).astype(BF16)
    row = lambda width: pl.BlockSpec((tm, width), lambda i: (i, 0))
    return pl.pallas_call(
        functools.partial(_ssd_proj_kernel, dz=dz, dxbc=dxbc),
        grid=(n // tm,),
        in_specs=[row(d), _const_spec((1, d)), _const_spec((d, w.shape[1]))],
        out_specs=[row(dz), row(dxbc), row(LANES)],
        out_shape=[jax.ShapeDtypeStruct((n, dz), F32), jax.ShapeDtypeStruct((n, dxbc), F32),
                   jax.ShapeDtypeStruct((n, LANES), F32)],
        compiler_params=_cparams("parallel"),
        name="ssd_proj",
    )(x2, g, w)


DSA_QB = 128
DSA_KC = 512
COUNT_CHAINS = 8


def _dsa_kernel(iqt_ref, iwt_ref, qt_ref, ik_ref, k_ref, vt_ref, tri_ref, o_ref,
                key_ref, m_ref, l_ref, acc_ref, s_ref, *, nsel):
    qb = pl.program_id(1)
    nchunk = qb // (DSA_KC // DSA_QB) + 1
    grp = ATTN_HEADS // ATTN_KV_HEADS
    idx_scale = IDX_HEAD_DIM ** -0.5
    w_scale = IDX_HEADS ** -0.5
    exp2_scale = ATTN_HEAD_DIM ** -0.5 * math.log2(math.e)

    t_glob = qb * DSA_QB + lax.broadcasted_iota(I32, (DSA_KC, DSA_QB), 1)
    s_loc = lax.broadcasted_iota(I32, (DSA_KC, DSA_QB), 0)
    t_glob_b = qb * DSA_QB + lax.broadcasted_iota(I32, (DSA_QB, DSA_QB), 1)
    s_loc_b = lax.broadcasted_iota(I32, (DSA_QB, DSA_QB), 0)
    iw_t = iwt_ref[...] * w_scale
    iq_all = jnp.concatenate([iqt_ref[h * LANES:(h + 1) * LANES, :] for h in range(IDX_HEADS)], axis=1)
    q_all = [jnp.concatenate([qt_ref[(kvh * grp + g) * LANES:(kvh * grp + g + 1) * LANES, :]
                              for g in range(grp)], axis=1) for kvh in range(ATTN_KV_HEADS)]

    def score_chunk(c, carry):
        ikc = ik_ref[pl.ds(c * DSA_KC, DSA_KC), :]
        d = jnp.dot(ikc, iq_all, preferred_element_type=F32)
        sc = jnp.zeros((DSA_KC, DSA_QB), F32)
        for h in range(IDX_HEADS):
            sc = sc + jnp.maximum(d[:, h * DSA_QB:(h + 1) * DSA_QB] * idx_scale, 0.0) * iw_t[h:h + 1, :]
        sc = jnp.where(c * DSA_KC + s_loc <= t_glob, sc, -jnp.inf)
        bits = pltpu.bitcast(sc, I32)
        key_ref[pl.ds(c * DSA_KC, DSA_KC), :] = jnp.where(bits < 0, bits ^ 0x7FFFFFFF, bits)
        return carry

    lax.fori_loop(0, nchunk, score_chunk, 0)

    def count_ge(cand):
        def body(c, accs):
            accs = list(accs)
            for j in range(DSA_KC // SUBLANES):
                blk = key_ref[pl.ds(c * DSA_KC + j * SUBLANES, SUBLANES), :]
                accs[j % COUNT_CHAINS] = accs[j % COUNT_CHAINS] + jnp.where(blk >= cand, 1, 0)
            return tuple(accs)
        zero = jnp.zeros((SUBLANES, DSA_QB), I32)
        accs = lax.fori_loop(0, nchunk, body, (zero,) * COUNT_CHAINS)
        return jnp.sum(functools.reduce(lambda x, y: x + y, accs), axis=0, keepdims=True)

    def bisect(it, ans):
        cand = ans + lax.shift_left(jnp.int32(1), 31 - it)
        cnt = count_ge(cand)
        return jnp.where(cnt >= nsel, cand, ans)

    tau = lax.fori_loop(0, 32, bisect, jnp.full((SUBLANES, DSA_QB), INT_MIN, I32))
    n_gt = count_ge(tau + 1)
    n_tie = (nsel - n_gt).astype(F32)
    tau_row = tau[0:1, :]

    m_ref[...] = jnp.full(m_ref.shape, NEG_BIG, F32)
    l_ref[...] = jnp.zeros(l_ref.shape, F32)
    acc_ref[...] = jnp.zeros(acc_ref.shape, F32)

    last = nchunk - 1

    def masked_logits(c, slot, tie_carry):
        cl = jnp.minimum(c, last)
        biases = []
        for sb in range(DSA_KC // DSA_QB):
            keys = key_ref[pl.ds(cl * DSA_KC + sb * DSA_QB, DSA_QB), :]
            eq = keys == tau_row
            prefix = jnp.dot(tri_ref[...], jnp.where(eq, 1.0, 0.0).astype(BF16),
                             preferred_element_type=F32) + tie_carry
            tie_carry = prefix[DSA_QB - 1:DSA_QB, :]
            sel = (keys > tau_row) | (eq & (prefix <= n_tie))
            sel = sel & (c * DSA_KC + sb * DSA_QB + s_loc_b <= t_glob_b)
            biases.append(jnp.where(sel, 0.0, NEG_BIG))
        bias = jnp.concatenate(biases, axis=0)
        bias = jnp.concatenate([bias] * grp, axis=1)
        for kvh in range(ATTN_KV_HEADS):
            kc = k_ref[pl.ds(cl * DSA_KC, DSA_KC), kvh * LANES:(kvh + 1) * LANES]
            s_ref[slot, kvh] = jnp.dot(kc, q_all[kvh], preferred_element_type=F32) + bias
        return tie_carry

    def softmax_pv(c, slot):
        cl = jnp.minimum(c, last)
        for kvh in range(ATTN_KV_HEADS):
            vtc = vt_ref[cl, kvh * LANES:(kvh + 1) * LANES, :]
            s = s_ref[slot, kvh]
            m_old = m_ref[kvh]
            m_new = jnp.maximum(m_old, jnp.max(s, axis=0, keepdims=True))
            p = jnp.exp2((s - m_new) * exp2_scale)
            alpha = jnp.exp2((m_old - m_new) * exp2_scale)
            l_ref[kvh] = alpha * l_ref[kvh] + jnp.sum(p, axis=0, keepdims=True)
            acc_ref[kvh] = alpha * acc_ref[kvh] + jnp.dot(vtc, p.astype(BF16), preferred_element_type=F32)
            m_ref[kvh] = m_new

    def attn_pair(k2, tie_carry):
        c = 2 * k2
        tie_carry = masked_logits(c + 1, 1, tie_carry)
        softmax_pv(c, 0)
        tie_carry = masked_logits(c + 2, 0, tie_carry)
        softmax_pv(c + 1, 1)
        return tie_carry

    tie0 = masked_logits(0, 0, jnp.zeros((1, DSA_QB), F32))
    lax.fori_loop(0, (nchunk + 1) // 2, attn_pair, tie0)
    for kvh in range(ATTN_KV_HEADS):
        o_t = acc_ref[kvh] / l_ref[kvh]
        for g in range(grp):
            h = kvh * grp + g
            o_ref[:, h * LANES:(h + 1) * LANES] = jnp.transpose(
                o_t[:, g * DSA_QB:(g + 1) * DSA_QB]).astype(o_ref.dtype)


def _dsa(qt, k, vt, iqt, ik, iwt, batch, seq):
    n = batch * seq
    nsel = min(TOPK_MAX, seq // 4)
    nqb = seq // DSA_QB
    grp_lanes = ATTN_HEADS // ATTN_KV_HEADS * DSA_QB
    hq = ATTN_HEADS * ATTN_HEAD_DIM
    hkv = ATTN_KV_HEADS * ATTN_HEAD_DIM
    tri = (jnp.arange(DSA_QB)[:, None] >= jnp.arange(DSA_QB)[None, :]).astype(BF16)
    qcol = lambda height: pl.BlockSpec((height, DSA_QB), lambda b, i: (0, b * nqb + i))
    full = lambda width: pl.BlockSpec((seq, width), lambda b, i: (b, 0))
    return pl.pallas_call(
        functools.partial(_dsa_kernel, nsel=nsel),
        grid=(batch, nqb),
        in_specs=[qcol(IDX_HEADS * LANES), qcol(SUBLANES), qcol(hq), full(LANES), full(hkv),
                  pl.BlockSpec((seq // DSA_KC, hkv, DSA_KC), lambda b, i: (b, 0, 0)),
                  _const_spec((DSA_QB, DSA_QB))],
        out_specs=pl.BlockSpec((DSA_QB, hq), lambda b, i: (b * nqb + i, 0)),
        out_shape=jax.ShapeDtypeStruct((n, hq), BF16),
        scratch_shapes=[pltpu.VMEM((seq, DSA_QB), I32),
                        pltpu.VMEM((ATTN_KV_HEADS, 1, grp_lanes), F32),
                        pltpu.VMEM((ATTN_KV_HEADS, 1, grp_lanes), F32),
                        pltpu.VMEM((ATTN_KV_HEADS, ATTN_HEAD_DIM, grp_lanes), F32),
                        pltpu.VMEM((2, ATTN_KV_HEADS, DSA_KC, grp_lanes), F32)],
        compiler_params=_cparams("arbitrary", "arbitrary"),
        name="dsa",
    )(iqt, iwt, qt, ik, k, vt, tri)


CONV_HALO = 8


def _ssd_kernel(xbc_ref, z_ref, dt_ref, cw_ref, cb_ref, dtb_ref, alog_ref, dskip_ref, ng_ref,
                tri_ref, exp_ref, o_ref, ext_ref, state_ref, *, d_inner):
    L = SSD_CHUNK
    gn = SSD_GROUPS * SSD_STATE
    heads_per_group = d_inner // SSD_HEAD_DIM // SSD_GROUPS
    gw = heads_per_group * SSD_HEAD_DIM

    @pl.when(pl.program_id(1) == 0)
    def _():
        ext_ref[0:CONV_HALO, :] = jnp.zeros((CONV_HALO, ext_ref.shape[1]), F32)
        state_ref[...] = jnp.zeros(state_ref.shape, F32)

    raw = xbc_ref[...]
    ext_ref[CONV_HALO:CONV_HALO + L, :] = raw
    acc = cb_ref[...] + jnp.zeros_like(raw)
    for kk in range(SSD_CONV):
        start = CONV_HALO - (SSD_CONV - 1) + kk
        acc = acc + cw_ref[kk:kk + 1, :] * ext_ref[start:start + L, :]
    ext_ref[0:CONV_HALO, :] = raw[L - CONV_HALO:L, :]
    xbc = acc * jax.nn.sigmoid(acc)
    xs = xbc[:, :d_inner]
    bm = xbc[:, d_inner:d_inner + gn]
    cm = xbc[:, d_inner + gn:]

    dt = jax.nn.softplus(dt_ref[...] + dtb_ref[...])
    a = -jnp.exp(alog_ref[...])
    acs = _dot_exact_lhs(tri_ref[...], dt * a)
    acs_t = jnp.transpose(acs)
    e_acs = jnp.exp(acs)
    e_end = jnp.exp(acs[L - 1:L, :] - acs)
    expand = exp_ref[...]
    dt_x = _dot_exact_rhs(dt, expand)
    e_acs_x = _dot_exact_rhs(e_acs, expand)
    e_end_x = _dot_exact_rhs(e_end, expand)
    xdt = xs * dt_x
    xdt_b = xdt.astype(BF16)
    xw_b = (xdt * e_end_x).astype(BF16)

    lower = lax.broadcasted_iota(I32, (L, L), 0) >= lax.broadcasted_iota(I32, (L, L), 1)
    lane = lax.broadcasted_iota(I32, (L, LANES), 1)
    heads_per_tile = LANES // SSD_HEAD_DIM
    for g in range(SSD_GROUPS):
        cg = cm[:, g * SSD_STATE:(g + 1) * SSD_STATE].astype(BF16)
        bg = bm[:, g * SSD_STATE:(g + 1) * SSD_STATE]
        cb = _dot_nt(cg, bg.astype(BF16))
        st = state_ref[:, g * gw:(g + 1) * gw]
        y_off = jnp.dot(cg, st.astype(BF16), preferred_element_type=F32) * e_acs_x[:, g * gw:(g + 1) * gw]
        for tile in range(gw // LANES):
            col = g * gw + tile * LANES
            x_tile = xdt_b[:, col:col + LANES]
            y_tile = y_off[:, tile * LANES:(tile + 1) * LANES]
            for sub in range(heads_per_tile):
                h = col // SSD_HEAD_DIM + sub
                seg = acs[:, h:h + 1] - acs_t[h:h + 1, :]
                m = (cb * jnp.exp(jnp.where(lower, seg, -jnp.inf))).astype(BF16)
                in_head = (lane >= sub * SSD_HEAD_DIM) & (lane < (sub + 1) * SSD_HEAD_DIM)
                y_tile = y_tile + jnp.dot(m, jnp.where(in_head, x_tile, jnp.zeros_like(x_tile)),
                                          preferred_element_type=F32)
            y_tile = y_tile + xs[:, col:col + LANES] * dskip_ref[:, col:col + LANES]
            zt = z_ref[:, col:col + LANES]
            o_ref[:, col:col + LANES] = (y_tile * (zt * jax.nn.sigmoid(zt))).astype(o_ref.dtype)
        bg_t = jnp.transpose(bg).astype(BF16)
        state_ref[:, g * gw:(g + 1) * gw] = (
            st * e_acs_x[L - 1:L, g * gw:(g + 1) * gw]
            + jnp.dot(bg_t, xw_b[:, g * gw:(g + 1) * gw], preferred_element_type=F32))

    y = o_ref[...].astype(F32)
    o_ref[...] = _rmsnorm(y, ng_ref[...]).astype(o_ref.dtype)


def _ssd(z, xbc, dt, conv_w, conv_b, dt_bias, a_log, d_skip, norm_g, batch, seq):
    n, d_inner = z.shape
    cch = xbc.shape[1]
    heads = d_inner // SSD_HEAD_DIM
    nc = seq // SSD_CHUNK
    L = SSD_CHUNK
    tri = (jnp.arange(L)[:, None] >= jnp.arange(L)[None, :]).astype(BF16)
    expand = (jnp.arange(LANES)[:, None] == (jnp.arange(d_inner)[None, :] // SSD_HEAD_DIM)).astype(BF16)
    pad_h = lambda v: jnp.pad(v.reshape(1, heads), ((0, 0), (0, LANES - heads)))
    row = lambda width: pl.BlockSpec((L, width), lambda b, c: (b * nc + c, 0))
    return pl.pallas_call(
        functools.partial(_ssd_kernel, d_inner=d_inner),
        grid=(batch, nc),
        in_specs=[row(cch), row(d_inner), row(LANES),
                  _const_spec((SSD_CONV, cch)), _const_spec((1, cch)), _const_spec((1, LANES)),
                  _const_spec((1, LANES)), _const_spec((1, d_inner)), _const_spec((1, d_inner)),
                  _const_spec((L, L)), _const_spec((LANES, d_inner))],
        out_specs=row(d_inner),
        out_shape=jax.ShapeDtypeStruct((n, d_inner), F32),
        scratch_shapes=[pltpu.VMEM((CONV_HALO + L, cch), F32), pltpu.VMEM((SSD_STATE, d_inner), F32)],
        compiler_params=_cparams("arbitrary", "arbitrary"),
        name="ssd",
    )(xbc, z, dt, conv_w.reshape(SSD_CONV, cch), conv_b.reshape(1, cch), pad_h(dt_bias), pad_h(a_log),
      jnp.repeat(d_skip, SSD_HEAD_DIM).reshape(1, d_inner), norm_g.reshape(1, d_inner), tri, expand)


def _merge_kernel(x_ref, attn_ref, ssd_ref, gate_ref, wab_ref, wsb_ref, wout_ref, gc_ref, wcq_ref,
                  kv_ref, wco_ref, gf_ref, h_ref, a_ref, at_ref, *, d_model):
    dot = functools.partial(jnp.dot, preferred_element_type=F32)
    br_a = dot(attn_ref[...], wab_ref[...])
    br_s = dot(ssd_ref[...].astype(BF16), wsb_ref[...])
    merged = gate_ref[:, :d_model] * br_a + gate_ref[:, d_model:] * br_s
    h = x_ref[...] + dot(merged.astype(BF16), wout_ref[...])

    qc = dot(_rmsnorm(h, gc_ref[...]).astype(BF16), wcq_ref[...]).astype(BF16)
    dh = d_model // MEM_HEADS
    outs = []
    for hd in range(MEM_HEADS):
        kk = kv_ref[:, hd * dh:(hd + 1) * dh]
        vv = kv_ref[:, d_model + hd * dh:d_model + (hd + 1) * dh]
        logits = _dot_nt(qc[:, hd * dh:(hd + 1) * dh], kk) * dh ** -0.5
        e = jnp.exp(logits - jnp.max(logits, axis=-1, keepdims=True))
        p = e / jnp.sum(e, axis=-1, keepdims=True)
        outs.append(dot(p.astype(BF16), vv))
    o = jnp.concatenate(outs, axis=-1)
    h = h + dot(o.astype(BF16), wco_ref[...])
    h_ref[...] = h
    a = _rmsnorm(h, gf_ref[...])
    a_ref[...] = a.astype(a_ref.dtype)
    at_ref[...] = jnp.transpose(a).astype(at_ref.dtype)


def _merge(x2, attn, ssd, gates, w_ab, w_sb, w_out, g_cross, w_cq, kv, w_co, g_ffn, batch, seq, tm):
    n, d = x2.shape
    mem_len = kv.shape[0] // batch
    per_b = seq // tm
    row = lambda width: pl.BlockSpec((tm, width), lambda i: (i, 0))
    return pl.pallas_call(
        functools.partial(_merge_kernel, d_model=d),
        grid=(n // tm,),
        in_specs=[row(d), row(attn.shape[1]), row(ssd.shape[1]), row(gates.shape[1]),
                  _const_spec(w_ab.shape), _const_spec(w_sb.shape), _const_spec(w_out.shape),
                  _const_spec((1, d)), _const_spec(w_cq.shape),
                  pl.BlockSpec((mem_len, kv.shape[1]), lambda i: (i // per_b, 0)),
                  _const_spec(w_co.shape), _const_spec((1, d))],
        out_specs=[row(d), row(d), pl.BlockSpec((d, tm), lambda i: (0, i))],
        out_shape=[jax.ShapeDtypeStruct((n, d), F32), jax.ShapeDtypeStruct((n, d), BF16),
                   jax.ShapeDtypeStruct((d, n), BF16)],
        compiler_params=_cparams("parallel"),
        name="merge",
    )(x2, attn, ssd, gates, w_ab, w_sb, w_out, g_cross, w_cq, kv, w_co, g_ffn)


def _top16_rows(s_t):
    nk, tn = s_t.shape
    key_iota = lax.broadcasted_iota(I32, (nk, tn), 0).astype(F32)
    slot_iota = lax.broadcasted_iota(I32, (PEER_TOPK, tn), 0)
    cur = s_t
    rank = jnp.full((nk, tn), float(PEER_TOPK), F32)
    vals = jnp.zeros((PEER_TOPK, tn), F32)
    for r in range(PEER_TOPK):
        m = jnp.max(cur, axis=0, keepdims=True)
        first = jnp.min(jnp.where(cur == m, key_iota, float(nk)), axis=0, keepdims=True)
        hit = key_iota == first
        rank = jnp.where(hit, float(r), rank)
        cur = jnp.where(hit, -jnp.inf, cur)
        vals = jnp.where(slot_iota == r, m, vals)
    return vals, rank


def _pair_merge(v1, v2):
    k, tn = v1.shape
    slot = lax.broadcasted_iota(I32, (k, tn), 0).astype(F32)
    slot_f = slot
    count = jnp.zeros((k, tn), F32)
    front = v1 + v2[0:1, :]
    best = front[0:1, :]
    z = jnp.zeros((1, tn), F32)
    for _ in range(PEER_TOPK):
        m = jnp.max(front, axis=0, keepdims=True)
        first = jnp.min(jnp.where(front == m, slot, float(k)), axis=0, keepdims=True)
        hit = slot == first
        z = z + jnp.exp(m - best)
        count = jnp.where(hit, count + 1.0, count)
        nxt = jnp.sum(jnp.where(hit, count, 0.0), axis=0, keepdims=True)
        v2n = jnp.sum(jnp.where(slot_f == nxt, v2, 0.0), axis=0, keepdims=True)
        v2n = jnp.where(nxt >= float(k), -jnp.inf, v2n)
        front = jnp.where(hit, v1 + v2n, front)
    return count, z


def _peer_route_kernel(a_ref, wq_ref, sk_ref, r2_ref, g2_ref, la_ref, c1_ref):
    qry = jnp.dot(a_ref[...], wq_ref[...], preferred_element_type=F32).astype(BF16)
    for h in range(PEER_HEADS):
        halves = []
        for p in range(2):
            hp = h * 2 + p
            s_t = _dot_nt(sk_ref[hp], qry[:, hp * PEER_KEY_DIM:(hp + 1) * PEER_KEY_DIM])
            vals, rank = _top16_rows(s_t)
            halves.append((s_t, vals, rank))
        (s1, v1, rank1), (s2, v2, rank2) = halves
        count, z = _pair_merge(v1, v2)
        la = jnp.zeros_like(rank1)
        for i in range(PEER_TOPK):
            la = la + jnp.where(rank1 == float(i), count[i:i + 1, :], 0.0)
        r2_ref[h] = rank2.astype(r2_ref.dtype)
        g2_ref[h] = (jnp.exp(s2 - v2[0:1, :]) / z).astype(g2_ref.dtype)
        la_ref[h] = la
        c1_ref[h] = jnp.exp(s1 - v1[0:1, :])


def _peer_route(a3, w_pq, sub_keys, tn):
    n, d = a3.shape
    nk = sub_keys.shape[1]
    tab = lambda dt: jax.ShapeDtypeStruct((PEER_HEADS, nk, n), dt)
    tspec = pl.BlockSpec((PEER_HEADS, nk, tn), lambda i: (0, 0, i))
    return pl.pallas_call(
        _peer_route_kernel,
        grid=(n // tn,),
        in_specs=[pl.BlockSpec((tn, d), lambda i: (i, 0)), _const_spec(w_pq.shape), _const_spec(sub_keys.shape)],
        out_specs=[tspec] * 4,
        out_shape=[tab(BF16), tab(BF16), tab(F32), tab(F32)],
        compiler_params=_cparams("parallel"),
        name="peer_route",
    )(a3, w_pq, sub_keys)


PEER_SUB = 2


def _gelu_exact(x):
    return 0.5 * x * (1.0 + lax.erf(x * (2.0 ** -0.5)))


def _peer_dense_kernel(at_ref, u_ref, vt_ref, r2_ref, g2_ref, la_ref, c1_ref, h_ref, gfin_ref,
                       o_ref, yt_ref, z_ref, *, na, final_norm):
    j = pl.program_id(1)

    @pl.when(j == 0)
    def _():
        yt_ref[...] = jnp.zeros(yt_ref.shape, F32)

    nk = PEER_N_KEYS
    tn = at_ref.shape[1]
    y = None
    nsub = na // PEER_SUB
    rows_of = lambda sub: slice(sub * PEER_SUB * nk, (sub + 1) * PEER_SUB * nk)
    z_of = lambda sub: jnp.dot(u_ref[rows_of(sub), :], at_ref[...], preferred_element_type=F32)
    z_ref[0] = z_of(0)
    for sub in range(nsub):
        rows = rows_of(sub)
        if sub + 1 < nsub:
            z_ref[(sub + 1) % 2] = z_of(sub + 1)
        z_t = z_ref[sub % 2]
        parts = []
        for ai in range(PEER_SUB):
            a_idx = j * na + sub * PEER_SUB + ai
            w = jnp.zeros((nk, tn), BF16)
            for h in range(PEER_HEADS):
                la = la_ref[h, pl.ds(a_idx, 1), :].astype(BF16)
                c1 = c1_ref[h, pl.ds(a_idx, 1), :].astype(BF16)
                w = w + jnp.where(r2_ref[h] < la, g2_ref[h], jnp.zeros((), BF16)) * c1
            parts.append(w * _gelu_exact(z_t[ai * nk:(ai + 1) * nk, :]).astype(BF16))
        y_sub = jnp.dot(vt_ref[:, rows], jnp.concatenate(parts, axis=0), preferred_element_type=F32)
        y = y_sub if y is None else y + y_sub
    yt_ref[...] += y

    @pl.when(j == pl.num_programs(1) - 1)
    def _():
        h = h_ref[...] + jnp.transpose(yt_ref[...])
        o_ref[...] = _rmsnorm(h, gfin_ref[...]) if final_norm else h


def _peer_dense(a3_t, u_bf, v_t, tabs, h2, g_final, tn, na, final_norm):
    d, n = a3_t.shape
    n_exp = u_bf.shape[0]
    nk = PEER_N_KEYS
    tspec = pl.BlockSpec((PEER_HEADS, nk, tn), lambda i, j: (0, 0, i))
    return pl.pallas_call(
        functools.partial(_peer_dense_kernel, na=na, final_norm=final_norm),
        grid=(n // tn, n_exp // (na * nk)),
        in_specs=[pl.BlockSpec((d, tn), lambda i, j: (0, i)),
                  pl.BlockSpec((na * nk, d), lambda i, j: (j, 0)),
                  pl.BlockSpec((d, na * nk), lambda i, j: (0, j)),
                  tspec, tspec, tspec, tspec,
                  pl.BlockSpec((tn, d), lambda i, j: (i, 0)),
                  _const_spec((1, d))],
        out_specs=pl.BlockSpec((tn, d), lambda i, j: (i, 0)),
        out_shape=jax.ShapeDtypeStruct((n, d), F32),
        scratch_shapes=[pltpu.VMEM((d, tn), F32), pltpu.VMEM((2, PEER_SUB * nk, tn), F32)],
        compiler_params=_cparams("parallel", "arbitrary"),
        name="peer_dense",
    )(a3_t, u_bf, v_t, *tabs, h2, g_final)


def _split_w_in(w_in, d_model, d_inner, conv_ch, ssd_heads):
    sizes = (ATTN_HEADS * ATTN_HEAD_DIM, ATTN_KV_HEADS * ATTN_HEAD_DIM, ATTN_KV_HEADS * ATTN_HEAD_DIM,
             IDX_HEADS * IDX_HEAD_DIM, IDX_HEAD_DIM, IDX_HEADS, d_inner, conv_ch, ssd_heads, 2 * d_model)
    assert sum(sizes) == w_in.shape[1]
    attn_w = sum(sizes[:6])
    o = attn_w
    w_z = w_in[:, o:o + d_inner]; o += d_inner
    w_xbc = w_in[:, o:o + conv_ch]; o += conv_ch
    w_dt = w_in[:, o:o + ssd_heads]; o += ssd_heads
    w_gate = w_in[:, o:]
    return w_in[:, :attn_w], w_z, w_xbc, w_dt, w_gate


def kernel(x, mem, positions, norm_mix_g, w_in, conv_w, conv_b, dt_bias, a_log, d_skip, ssd_norm_g, w_attn_branch, w_ssd_branch, w_out, norm_cross_g, norm_mem_g, w_cross_q, w_cross_kv, w_cross_out, norm_ffn_g, w_peer_q, peer_sub_keys, peer_u, peer_v, norm_final_g):
    batch, seq, d = x.shape
    n = batch * seq
    depth = w_in.shape[0]
    d_inner = ssd_norm_g.shape[1]
    conv_ch = conv_b.shape[1]
    ssd_heads = dt_bias.shape[1]
    tm = 256
    row = lambda v: v.reshape(1, -1)

    h = x.reshape(n, d)
    pos = positions.reshape(n, 1).astype(F32)
    mem2 = mem.reshape(batch * mem.shape[1], d)
    a3 = tabs = None
    for layer in range(depth):
        w_attn, w_z, w_xbc, w_dt, w_gate = _split_w_in(w_in[layer], d, d_inner, conv_ch, ssd_heads)
        g_mix = row(norm_mix_g[layer])
        qt, k, vt, iqt, ik, iwt = _attn_proj(h, g_mix, w_attn, pos)
        z, xbc, dt = _ssd_proj(h, g_mix, w_z, w_xbc, w_dt, tm)
        gates = _rms_proj(_gate_proj_kernel, h, g_mix, w_gate.astype(BF16), F32, tm, "gate_proj")
        attn = _dsa(qt, k, vt, iqt, ik, iwt, batch, seq)
        ssd = _ssd(z, xbc, dt, conv_w[layer], conv_b[layer], dt_bias[layer], a_log[layer], d_skip[layer],
                   ssd_norm_g[layer], batch, seq)
        kv = _rms_proj(_plain_proj_kernel, mem2, row(norm_mem_g[layer]), w_cross_kv[layer].astype(BF16),
                       BF16, mem.shape[1], "mem_kv")
        h2, a3, a3_t = _merge(h, attn, ssd, gates, w_attn_branch[layer].astype(BF16), w_ssd_branch[layer].astype(BF16),
                        w_out[layer].astype(BF16), row(norm_cross_g[layer]), w_cross_q[layer].astype(BF16), kv,
                        w_cross_out[layer].astype(BF16), row(norm_ffn_g[layer]), batch, seq, tm)
        sk = peer_sub_keys[layer].reshape(PEER_HEADS * 2, PEER_N_KEYS, PEER_KEY_DIM).astype(BF16)
        tabs = _peer_route(a3, w_peer_q[layer].astype(BF16), sk, 256)
        h = _peer_dense(a3_t, peer_u[layer].astype(BF16), peer_v[layer].T.astype(BF16), tabs, h2,
                        row(norm_final_g), 512, 8, layer == depth - 1)
    return h.reshape(batch, seq, d)
```

```python
import functools
import math

import jax
import jax.numpy as jnp
from jax import lax
from jax.experimental import pallas as pl
from jax.experimental.pallas import tpu as pltpu

F32 = jnp.float32
BF16 = jnp.bfloat16
I32 = jnp.int32

ATTN_HEADS = 8
ATTN_KV_HEADS = 2
ATTN_HEAD_DIM = 128
IDX_HEADS = 4
IDX_HEAD_DIM = 64
TOPK_MAX = 256
ROPE_THETA = 500000.0
ROPE_FRACTION = 4
SSD_HEAD_DIM = 64
SSD_GROUPS = 4
SSD_STATE = 128
SSD_CONV = 4
SSD_CHUNK = 128
MEM_HEADS = 4
PEER_HEADS = 8
PEER_N_KEYS = 128
PEER_KEY_DIM = 128
PEER_TOPK = 16
EPS = 1e-6

LANES = 128
SUBLANES = 8
VMEM_LIMIT = 56 * 1024 * 1024

NEG_BIG = -1e30
INT_MIN = -2 ** 31


def _cparams(*sem):
    return pltpu.CompilerParams(dimension_semantics=sem, vmem_limit_bytes=VMEM_LIMIT)


def _const_spec(shape):
    nd = len(shape)
    return pl.BlockSpec(shape, lambda *_: (0,) * nd, pipeline_mode=pl.Buffered(1))


def _rmsnorm(x, g):
    return x * lax.rsqrt(jnp.mean(x * x, axis=-1, keepdims=True) + EPS) * g


def _split3(a):
    hi = a.astype(BF16)
    r1 = a - hi.astype(F32)
    mid = r1.astype(BF16)
    lo = (r1 - mid.astype(F32)).astype(BF16)
    return hi, mid, lo


def _dot_exact_rhs(a, b_bf16):
    hi, mid, lo = _split3(a)
    d = functools.partial(jnp.dot, preferred_element_type=F32)
    return d(hi, b_bf16) + d(mid, b_bf16) + d(lo, b_bf16)


def _dot_exact_lhs(a_bf16, b):
    hi, mid, lo = _split3(b)
    d = functools.partial(jnp.dot, preferred_element_type=F32)
    return d(a_bf16, hi) + d(a_bf16, mid) + d(a_bf16, lo)


def _dot_nt(a, b):
    return lax.dot_general(a, b, (((1,), (1,)), ((), ())), preferred_element_type=F32)


def _rope(x, cos, sin_signed, first_half, half):
    w = x.shape[-1]
    x_sw = jnp.where(first_half, pltpu.roll(x, w - half, 1), pltpu.roll(x, half, 1))
    return x * cos + x_sw * sin_signed


def _attn_proj_kernel(x_ref, g_ref, w_ref, pos_ref, rope_ref,
                      qt_ref, k_ref, vt_ref, iqt_ref, ik_ref, iwt_ref):
    a = _rmsnorm(x_ref[...], g_ref[...]).astype(BF16)
    p = jnp.dot(a, w_ref[...], preferred_element_type=F32)
    pos = pos_ref[...]
    tabs = []
    for r, half in ((0, ATTN_HEAD_DIM // ROPE_FRACTION // 2), (2, IDX_HEAD_DIM // ROPE_FRACTION // 2)):
        ang = pos * rope_ref[r:r + 1, :]
        sgn = rope_ref[r + 1:r + 2, :]
        tabs.append((jnp.cos(ang), jnp.sin(ang) * sgn, sgn < 0.0, half))
    t128, t64 = tabs
    off = 0
    for h in range(ATTN_HEADS):
        qt_ref[h * LANES:(h + 1) * LANES, :] = jnp.transpose(_rope(p[:, off:off + LANES], *t128)).astype(BF16)
        off += LANES
    for h in range(ATTN_KV_HEADS):
        k_ref[:, h * LANES:(h + 1) * LANES] = _rope(p[:, off:off + LANES], *t128).astype(BF16)
        off += LANES
    for h in range(ATTN_KV_HEADS):
        vt_ref[0, h * LANES:(h + 1) * LANES, :] = jnp.transpose(p[:, off:off + LANES]).astype(BF16)
        off += LANES
    for h in range(IDX_HEADS):
        iqt_ref[h * LANES:(h + 1) * LANES, :] = jnp.transpose(_rope(p[:, off:off + LANES], *t64)).astype(BF16)
        off += LANES
    ik_ref[...] = _rope(p[:, off:off + LANES], *t64).astype(BF16)
    off += LANES
    iwt_ref[...] = jnp.transpose(p[:, off:off + LANES])[0:SUBLANES, :]


def _rope_rows():
    rows = []
    for head_dim in (ATTN_HEAD_DIM, IDX_HEAD_DIM):
        rot = head_dim // ROPE_FRACTION
        half = rot // 2
        inv = ROPE_THETA ** (-2.0 * jnp.arange(half, dtype=F32) / rot)
        inv_row = jnp.zeros((LANES,), F32).at[:rot].set(jnp.concatenate([inv, inv]))
        sgn_row = jnp.zeros((LANES,), F32).at[:half].set(-1.0).at[half:rot].set(1.0)
        rows += [inv_row, sgn_row]
    return jnp.stack(rows)


def _pad_cols(w, width):
    return jnp.pad(w, ((0, 0), (0, width - w.shape[1])))


def _attn_proj(x2, g, w_in, pos_f32):
    n, d = x2.shape
    tm = DSA_KC
    hq = ATTN_HEADS * ATTN_HEAD_DIM
    hkv = ATTN_KV_HEADS * ATTN_HEAD_DIM
    o = 0
    wq = w_in[:, o:o + hq]; o += hq
    wk = w_in[:, o:o + hkv]; o += hkv
    wv = w_in[:, o:o + hkv]; o += hkv
    wiq = w_in[:, o:o + IDX_HEADS * IDX_HEAD_DIM]; o += IDX_HEADS * IDX_HEAD_DIM
    wik = w_in[:, o:o + IDX_HEAD_DIM]; o += IDX_HEAD_DIM
    wiw = w_in[:, o:o + IDX_HEADS]; o += IDX_HEADS
    wiq = jnp.pad(wiq.reshape(d, IDX_HEADS, IDX_HEAD_DIM),
                  ((0, 0), (0, 0), (0, LANES - IDX_HEAD_DIM))).reshape(d, IDX_HEADS * LANES)
    w = jnp.concatenate([wq, wk, wv, wiq, _pad_cols(wik, LANES), _pad_cols(wiw, LANES)], axis=1).astype(BF16)
    nw = w.shape[1]
    row = lambda width: pl.BlockSpec((tm, width), lambda i: (i, 0))
    col = lambda height: pl.BlockSpec((height, tm), lambda i: (0, i))
    outs = pl.pallas_call(
        _attn_proj_kernel,
        grid=(n // tm,),
        in_specs=[row(d), _const_spec((1, d)), _const_spec((d, nw)), row(1), _const_spec((4, LANES))],
        out_specs=[col(hq), row(hkv), pl.BlockSpec((1, hkv, tm), lambda i: (i, 0, 0)),
                   col(IDX_HEADS * LANES), row(LANES), col(SUBLANES)],
        out_shape=[jax.ShapeDtypeStruct((hq, n), BF16), jax.ShapeDtypeStruct((n, hkv), BF16),
                   jax.ShapeDtypeStruct((n // tm, hkv, tm), BF16),
                   jax.ShapeDtypeStruct((IDX_HEADS * LANES, n), BF16),
                   jax.ShapeDtypeStruct((n, LANES), BF16), jax.ShapeDtypeStruct((SUBLANES, n), F32)],
        compiler_params=_cparams("parallel"),
        name="attn_proj",
    )(x2, g, w, pos_f32, _rope_rows())
    return outs


def _ssd_proj_kernel(x_ref, g_ref, w_ref, z_ref, xbc_ref, dt_ref, *, dz, dxbc):
    a = _rmsnorm(x_ref[...], g_ref[...]).astype(BF16)
    z_ref[...] = jnp.dot(a, w_ref[:, :dz], preferred_element_type=F32)
    xbc_ref[...] = jnp.dot(a, w_ref[:, dz:dz + dxbc], preferred_element_type=F32)
    dt_ref[...] = jnp.dot(a, w_ref[:, dz + dxbc:], preferred_element_type=F32)


def _gate_proj_kernel(x_ref, g_ref, w_ref, o_ref):
    a = _rmsnorm(x_ref[...], g_ref[...]).astype(BF16)
    o_ref[...] = jax.nn.sigmoid(jnp.dot(a, w_ref[...], preferred_element_type=F32))


def _plain_proj_kernel(x_ref, g_ref, w_ref, o_ref):
    a = _rmsnorm(x_ref[...], g_ref[...]).astype(BF16)
    o_ref[...] = jnp.dot(a, w_ref[...], preferred_element_type=F32).astype(o_ref.dtype)


def _rms_proj(body, x2, g, w, out_dtype, tm, name):
    n, d = x2.shape
    nw = w.shape[1]
    return pl.pallas_call(
        body,
        grid=(n // tm,),
        in_specs=[pl.BlockSpec((tm, d), lambda i: (i, 0)), _const_spec((1, d)), _const_spec((d, nw))],
        out_specs=pl.BlockSpec((tm, nw), lambda i: (i, 0)),
        out_shape=jax.ShapeDtypeStruct((n, nw), out_dtype),
        compiler_params=_cparams("parallel"),
        name=name,
    )(x2, g, w)


def _ssd_proj(x2, g, w_z, w_xbc, w_dt, tm):
    n, d = x2.shape
    dz, dxbc = w_z.shape[1], w_xbc.shape[1]
    w = jnp.concatenate([w_z, w_xbc, _pad_cols(w_dt, LANES)], axis=1).astype(BF16)
    row = lambda width: pl.BlockSpec((tm, width), lambda i: (i, 0))
    return pl.pallas_call(
        functools.partial(_ssd_proj_kernel, dz=dz, dxbc=dxbc),
        grid=(n // tm,),
        in_specs=[row(d), _const_spec((1, d)), _const_spec((d, w.shape[1]))],
        out_specs=[row(dz), row(dxbc), row(LANES)],
        out_shape=[jax.ShapeDtypeStruct((n, dz), F32), jax.ShapeDtypeStruct((n, dxbc), F32),
                   jax.ShapeDtypeStruct((n, LANES), F32)],
        compiler_params=_cparams("parallel"),
        name="ssd_proj",
    )(x2, g, w)


DSA_QB = 128
DSA_KC = 512
COUNT_CHAINS = 8


def _dsa_kernel(iqt_ref, iwt_ref, qt_ref, ik_ref, k_ref, vt_ref, tri_ref, o_ref,
                key_ref, m_ref, l_ref, acc_ref, s_ref, *, nsel):
    qb = pl.program_id(1)
    nchunk = qb // (DSA_KC // DSA_QB) + 1
    grp = ATTN_HEADS // ATTN_KV_HEADS
    idx_scale = IDX_HEAD_DIM ** -0.5
    w_scale = IDX_HEADS ** -0.5
    exp2_scale = ATTN_HEAD_DIM ** -0.5 * math.log2(math.e)

    t_glob = qb * DSA_QB + lax.broadcasted_iota(I32, (DSA_KC, DSA_QB), 1)
    s_loc = lax.broadcasted_iota(I32, (DSA_KC, DSA_QB), 0)
    t_glob_b = qb * DSA_QB + lax.broadcasted_iota(I32, (DSA_QB, DSA_QB), 1)
    s_loc_b = lax.broadcasted_iota(I32, (DSA_QB, DSA_QB), 0)
    iw_t = iwt_ref[...] * w_scale
    iq_all = jnp.concatenate([iqt_ref[h * LANES:(h + 1) * LANES, :] for h in range(IDX_HEADS)], axis=1)
    q_all = [jnp.concatenate([qt_ref[(kvh * grp + g) * LANES:(kvh * grp + g + 1) * LANES, :]
                              for g in range(grp)], axis=1) for kvh in range(ATTN_KV_HEADS)]

    def score_chunk(c, carry):
        ikc = ik_ref[pl.ds(c * DSA_KC, DSA_KC), :]
        d = jnp.dot(ikc, iq_all, preferred_element_type=F32)
        sc = jnp.zeros((DSA_KC, DSA_QB), F32)
        for h in range(IDX_HEADS):
            sc = sc + jnp.maximum(d[:, h * DSA_QB:(h + 1) * DSA_QB] * idx_scale, 0.0) * iw_t[h:h + 1, :]
        sc = jnp.where(c * DSA_KC + s_loc <= t_glob, sc, -jnp.inf)
        bits = pltpu.bitcast(sc, I32)
        key_ref[pl.ds(c * DSA_KC, DSA_KC), :] = jnp.where(bits < 0, bits ^ 0x7FFFFFFF, bits)
        return carry

    lax.fori_loop(0, nchunk, score_chunk, 0)

    def count_ge(cand):
        def body(c, accs):
            accs = list(accs)
            for j in range(DSA_KC // SUBLANES):
                blk = key_ref[pl.ds(c * DSA_KC + j * SUBLANES, SUBLANES), :]
                accs[j % COUNT_CHAINS] = accs[j % COUNT_CHAINS] + jnp.where(blk >= cand, 1, 0)
            return tuple(accs)
        zero = jnp.zeros((SUBLANES, DSA_QB), I32)
        accs = lax.fori_loop(0, nchunk, body, (zero,) * COUNT_CHAINS)
        return jnp.sum(functools.reduce(lambda x, y: x + y, accs), axis=0, keepdims=True)

    def bisect(it, ans):
        cand = ans + lax.shift_left(jnp.int32(1), 31 - it)
        cnt = count_ge(cand)
        return jnp.where(cnt >= nsel, cand, ans)

    tau = lax.fori_loop(0, 32, bisect, jnp.full((SUBLANES, DSA_QB), INT_MIN, I32))
    n_gt = count_ge(tau + 1)
    n_tie = (nsel - n_gt).astype(F32)
    tau_row = tau[0:1, :]

    m_ref[...] = jnp.full(m_ref.shape, NEG_BIG, F32)
    l_ref[...] = jnp.zeros(l_ref.shape, F32)
    acc_ref[...] = jnp.zeros(acc_ref.shape, F32)

    last = nchunk - 1

    def masked_logits(c, slot, tie_carry):
        cl = jnp.minimum(c, last)
        biases = []
        for sb in range(DSA_KC // DSA_QB):
            keys = key_ref[pl.ds(cl * DSA_KC + sb * DSA_QB, DSA_QB), :]
            eq = keys == tau_row
            prefix = jnp.dot(tri_ref[...], jnp.where(eq, 1.0, 0.0).astype(BF16),
                             preferred_element_type=F32) + tie_carry
            tie_carry = prefix[DSA_QB - 1:DSA_QB, :]
            sel = (keys > tau_row) | (eq & (prefix <= n_tie))
            sel = sel & (c * DSA_KC + sb * DSA_QB + s_loc_b <= t_glob_b)
            biases.append(jnp.where(sel, 0.0, NEG_BIG))
        bias = jnp.concatenate(biases, axis=0)
        bias = jnp.concatenate([bias] * grp, axis=1)
        for kvh in range(ATTN_KV_HEADS):
            kc = k_ref[pl.ds(cl * DSA_KC, DSA_KC), kvh * LANES:(kvh + 1) * LANES]
            s_ref[slot, kvh] = jnp.dot(kc, q_all[kvh], preferred_element_type=F32) + bias
        return tie_carry

    def softmax_pv(c, slot):
        cl = jnp.minimum(c, last)
        for kvh in range(ATTN_KV_HEADS):
            vtc = vt_ref[cl, kvh * LANES:(kvh + 1) * LANES, :]
            s = s_ref[slot, kvh]
            m_old = m_ref[kvh]
            m_new = jnp.maximum(m_old, jnp.max(s, axis=0, keepdims=True))
            p = jnp.exp2((s - m_new) * exp2_scale)
            alpha = jnp.exp2((m_old - m_new) * exp2_scale)
            l_ref[kvh] = alpha * l_ref[kvh] + jnp.sum(p, axis=0, keepdims=True)
            acc_ref[kvh] = alpha * acc_ref[kvh] + jnp.dot(vtc, p.astype(BF16), preferred_element_type=F32)
            m_ref[kvh] = m_new

    def attn_pair(k2, tie_carry):
        c = 2 * k2
        tie_carry = masked_logits(c + 1, 1, tie_carry)
        softmax_pv(c, 0)
        tie_carry = masked_logits(c + 2, 0, tie_carry)
        softmax_pv(c + 1, 1)
        return tie_carry

    tie0 = masked_logits(0, 0, jnp.zeros((1, DSA_QB), F32))
    lax.fori_loop(0, (nchunk + 1) // 2, attn_pair, tie0)
    for kvh in range(ATTN_KV_HEADS):
        o_t = acc_ref[kvh] / l_ref[kvh]
        for g in range(grp):
            h = kvh * grp + g
            o_ref[:, h * LANES:(h + 1) * LANES] = jnp.transpose(
                o_t[:, g * DSA_QB:(g + 1) * DSA_QB]).astype(o_ref.dtype)


def _dsa(qt, k, vt, iqt, ik, iwt, batch, seq):
    n = batch * seq
    nsel = min(TOPK_MAX, seq // 4)
    nqb = seq // DSA_QB
    grp_lanes = ATTN_HEADS // ATTN_KV_HEADS * DSA_QB
    hq = ATTN_HEADS * ATTN_HEAD_DIM
    hkv = ATTN_KV_HEADS * ATTN_HEAD_DIM
    tri = (jnp.arange(DSA_QB)[:, None] >= jnp.arange(DSA_QB)[None, :]).astype(BF16)
    qcol = lambda height: pl.BlockSpec((height, DSA_QB), lambda b, i: (0, b * nqb + i))
    full = lambda width: pl.BlockSpec((seq, width), lambda b, i: (b, 0))
    return pl.pallas_call(
        functools.partial(_dsa_kernel, nsel=nsel),
        grid=(batch, nqb),
        in_specs=[qcol(IDX_HEADS * LANES), qcol(SUBLANES), qcol(hq), full(LANES), full(hkv),
                  pl.BlockSpec((seq // DSA_KC, hkv, DSA_KC), lambda b, i: (b, 0, 0)),
                  _const_spec((DSA_QB, DSA_QB))],
        out_specs=pl.BlockSpec((DSA_QB, hq), lambda b, i: (b * nqb + i, 0)),
        out_shape=jax.ShapeDtypeStruct((n, hq), BF16),
        scratch_shapes=[pltpu.VMEM((seq, DSA_QB), I32),
                        pltpu.VMEM((ATTN_KV_HEADS, 1, grp_lanes), F32),
                        pltpu.VMEM((ATTN_KV_HEADS, 1, grp_lanes), F32),
                        pltpu.VMEM((ATTN_KV_HEADS, ATTN_HEAD_DIM, grp_lanes), F32),
                        pltpu.VMEM((2, ATTN_KV_HEADS, DSA_KC, grp_lanes), F32)],
        compiler_params=_cparams("arbitrary", "arbitrary"),
        name="dsa",
    )(iqt, iwt, qt, ik, k, vt, tri)


CONV_HALO = 8


def _ssd_kernel(xbc_ref, z_ref, dt_ref, cw_ref, cb_ref, dtb_ref, alog_ref, dskip_ref, ng_ref,
                tri_ref, exp_ref, o_ref, ext_ref, state_ref, *, d_inner):
    L = SSD_CHUNK
    gn = SSD_GROUPS * SSD_STATE
    heads_per_group = d_inner // SSD_HEAD_DIM // SSD_GROUPS
    gw = heads_per_group * SSD_HEAD_DIM

    @pl.when(pl.program_id(1) == 0)
    def _():
        ext_ref[0:CONV_HALO, :] = jnp.zeros((CONV_HALO, ext_ref.shape[1]), F32)
        state_ref[...] = jnp.zeros(state_ref.shape, F32)

    raw = xbc_ref[...]
    ext_ref[CONV_HALO:CONV_HALO + L, :] = raw
    acc = cb_ref[...] + jnp.zeros_like(raw)
    for kk in range(SSD_CONV):
        start = CONV_HALO - (SSD_CONV - 1) + kk
        acc = acc + cw_ref[kk:kk + 1, :] * ext_ref[start:start + L, :]
    ext_ref[0:CONV_HALO, :] = raw[L - CONV_HALO:L, :]
    xbc = acc * jax.nn.sigmoid(acc)
    xs = xbc[:, :d_inner]
    bm = xbc[:, d_inner:d_inner + gn]
    cm = xbc[:, d_inner + gn:]

    dt = jax.nn.softplus(dt_ref[...] + dtb_ref[...])
    a = -jnp.exp(alog_ref[...])
    acs = _dot_exact_lhs(tri_ref[...], dt * a)
    acs_t = jnp.transpose(acs)
    e_acs = jnp.exp(acs)
    e_end = jnp.exp(acs[L - 1:L, :] - acs)
    expand = exp_ref[...]
    dt_x = _dot_exact_rhs(dt, expand)
    e_acs_x = _dot_exact_rhs(e_acs, expand)
    e_end_x = _dot_exact_rhs(e_end, expand)
    xdt = xs * dt_x
    xdt_b = xdt.astype(BF16)
    xw_b = (xdt * e_end_x).astype(BF16)

    lower = lax.broadcasted_iota(I32, (L, L), 0) >= lax.broadcasted_iota(I32, (L, L), 1)
    lane = lax.broadcasted_iota(I32, (L, LANES), 1)
    heads_per_tile = LANES // SSD_HEAD_DIM
    for g in range(SSD_GROUPS):
        cg = cm[:, g * SSD_STATE:(g + 1) * SSD_STATE].astype(BF16)
        bg = bm[:, g * SSD_STATE:(g + 1) * SSD_STATE]
        cb = _dot_nt(cg, bg.astype(BF16))
        st = state_ref[:, g * gw:(g + 1) * gw]
        y_off = jnp.dot(cg, st.astype(BF16), preferred_element_type=F32) * e_acs_x[:, g * gw:(g + 1) * gw]
        for tile in range(gw // LANES):
            col = g * gw + tile * LANES
            x_tile = xdt_b[:, col:col + LANES]
            y_tile = y_off[:, tile * LANES:(tile + 1) * LANES]
            for sub in range(heads_per_tile):
                h = col // SSD_HEAD_DIM + sub
                seg = acs[:, h:h + 1] - acs_t[h:h + 1, :]
                m = (cb * jnp.exp(jnp.where(lower, seg, -jnp.inf))).astype(BF16)
                in_head = (lane >= sub * SSD_HEAD_DIM) & (lane < (sub + 1) * SSD_HEAD_DIM)
                y_tile = y_tile + jnp.dot(m, jnp.where(in_head, x_tile, jnp.zeros_like(x_tile)),
                                          preferred_element_type=F32)
            y_tile = y_tile + xs[:, col:col + LANES] * dskip_ref[:, col:col + LANES]
            zt = z_ref[:, col:col + LANES]
            o_ref[:, col:col + LANES] = (y_tile * (zt * jax.nn.sigmoid(zt))).astype(o_ref.dtype)
        bg_t = jnp.transpose(bg).astype(BF16)
        state_ref[:, g * gw:(g + 1) * gw] = (
            st * e_acs_x[L - 1:L, g * gw:(g + 1) * gw]
            + jnp.dot(bg_t, xw_b[:, g * gw:(g + 1) * gw], preferred_element_type=F32))

    y = o_ref[...].astype(F32)
    o_ref[...] = _rmsnorm(y, ng_ref[...]).astype(o_ref.dtype)


def _ssd(z, xbc, dt, conv_w, conv_b, dt_bias, a_log, d_skip, norm_g, batch, seq):
    n, d_inner = z.shape
    cch = xbc.shape[1]
    heads = d_inner // SSD_HEAD_DIM
    nc = seq // SSD_CHUNK
    L = SSD_CHUNK
    tri = (jnp.arange(L)[:, None] >= jnp.arange(L)[None, :]).astype(BF16)
    expand = (jnp.arange(LANES)[:, None] == (jnp.arange(d_inner)[None, :] // SSD_HEAD_DIM)).astype(BF16)
    pad_h = lambda v: jnp.pad(v.reshape(1, heads), ((0, 0), (0, LANES - heads)))
    row = lambda width: pl.BlockSpec((L, width), lambda b, c: (b * nc + c, 0))
    return pl.pallas_call(
        functools.partial(_ssd_kernel, d_inner=d_inner),
        grid=(batch, nc),
        in_specs=[row(cch), row(d_inner), row(LANES),
                  _const_spec((SSD_CONV, cch)), _const_spec((1, cch)), _const_spec((1, LANES)),
                  _const_spec((1, LANES)), _const_spec((1, d_inner)), _const_spec((1, d_inner)),
                  _const_spec((L, L)), _const_spec((LANES, d_inner))],
        out_specs=row(d_inner),
        out_shape=jax.ShapeDtypeStruct((n, d_inner), F32),
        scratch_shapes=[pltpu.VMEM((CONV_HALO + L, cch), F32), pltpu.VMEM((SSD_STATE, d_inner), F32)],
        compiler_params=_cparams("arbitrary", "arbitrary"),
        name="ssd",
    )(xbc, z, dt, conv_w.reshape(SSD_CONV, cch), conv_b.reshape(1, cch), pad_h(dt_bias), pad_h(a_log),
      jnp.repeat(d_skip, SSD_HEAD_DIM).reshape(1, d_inner), norm_g.reshape(1, d_inner), tri, expand)


def _merge_kernel(x_ref, attn_ref, ssd_ref, gate_ref, wab_ref, wsb_ref, wout_ref, gc_ref, wcq_ref,
                  kv_ref, wco_ref, gf_ref, h_ref, a_ref, at_ref, *, d_model):
    dot = functools.partial(jnp.dot, preferred_element_type=F32)
    br_a = dot(attn_ref[...], wab_ref[...])
    br_s = dot(ssd_ref[...].astype(BF16), wsb_ref[...])
    merged = gate_ref[:, :d_model] * br_a + gate_ref[:, d_model:] * br_s
    h = x_ref[...] + dot(merged.astype(BF16), wout_ref[...])

    qc = dot(_rmsnorm(h, gc_ref[...]).astype(BF16), wcq_ref[...]).astype(BF16)
    dh = d_model // MEM_HEADS
    outs = []
    for hd in range(MEM_HEADS):
        kk = kv_ref[:, hd * dh:(hd + 1) * dh]
        vv = kv_ref[:, d_model + hd * dh:d_model + (hd + 1) * dh]
        logits = _dot_nt(qc[:, hd * dh:(hd + 1) * dh], kk) * dh ** -0.5
        e = jnp.exp(logits - jnp.max(logits, axis=-1, keepdims=True))
        p = e / jnp.sum(e, axis=-1, keepdims=True)
        outs.append(dot(p.astype(BF16), vv))
    o = jnp.concatenate(outs, axis=-1)
    h = h + dot(o.astype(BF16), wco_ref[...])
    h_ref[...] = h
    a = _rmsnorm(h, gf_ref[...])
    a_ref[...] = a.astype(a_ref.dtype)
    at_ref[...] = jnp.transpose(a).astype(at_ref.dtype)


def _merge(x2, attn, ssd, gates, w_ab, w_sb, w_out, g_cross, w_cq, kv, w_co, g_ffn, batch, seq, tm):
    n, d = x2.shape
    mem_len = kv.shape[0] // batch
    per_b = seq // tm
    row = lambda width: pl.BlockSpec((tm, width), lambda i: (i, 0))
    return pl.pallas_call(
        functools.partial(_merge_kernel, d_model=d),
        grid=(n // tm,),
        in_specs=[row(d), row(attn.shape[1]), row(ssd.shape[1]), row(gates.shape[1]),
                  _const_spec(w_ab.shape), _const_spec(w_sb.shape), _const_spec(w_out.shape),
                  _const_spec((1, d)), _const_spec(w_cq.shape),
                  pl.BlockSpec((mem_len, kv.shape[1]), lambda i: (i // per_b, 0)),
                  _const_spec(w_co.shape), _const_spec((1, d))],
        out_specs=[row(d), row(d), pl.BlockSpec((d, tm), lambda i: (0, i))],
        out_shape=[jax.ShapeDtypeStruct((n, d), F32), jax.ShapeDtypeStruct((n, d), BF16),
                   jax.ShapeDtypeStruct((d, n), BF16)],
        compiler_params=_cparams("parallel"),
        name="merge",
    )(x2, attn, ssd, gates, w_ab, w_sb, w_out, g_cross, w_cq, kv, w_co, g_ffn)


def _top16_rows(s_t):
    nk, tn = s_t.shape
    key_iota = lax.broadcasted_iota(I32, (nk, tn), 0).astype(F32)
    slot_iota = lax.broadcasted_iota(I32, (PEER_TOPK, tn), 0)
    cur = s_t
    rank = jnp.full((nk, tn), float(PEER_TOPK), F32)
    vals = jnp.zeros((PEER_TOPK, tn), F32)
    for r in range(PEER_TOPK):
        m = jnp.max(cur, axis=0, keepdims=True)
        first = jnp.min(jnp.where(cur == m, key_iota, float(nk)), axis=0, keepdims=True)
        hit = key_iota == first
        rank = jnp.where(hit, float(r), rank)
        cur = jnp.where(hit, -jnp.inf, cur)
        vals = jnp.where(slot_iota == r, m, vals)
    return vals, rank


def _pair_merge(v1, v2):
    k, tn = v1.shape
    slot = lax.broadcasted_iota(I32, (k, tn), 0).astype(F32)
    slot_f = slot
    count = jnp.zeros((k, tn), F32)
    front = v1 + v2[0:1, :]
    best = front[0:1, :]
    z = jnp.zeros((1, tn), F32)
    for _ in range(PEER_TOPK):
        m = jnp.max(front, axis=0, keepdims=True)
        first = jnp.min(jnp.where(front == m, slot, float(k)), axis=0, keepdims=True)
        hit = slot == first
        z = z + jnp.exp(m - best)
        count = jnp.where(hit, count + 1.0, count)
        nxt = jnp.sum(jnp.where(hit, count, 0.0), axis=0, keepdims=True)
        v2n = jnp.sum(jnp.where(slot_f == nxt, v2, 0.0), axis=0, keepdims=True)
        v2n = jnp.where(nxt >= float(k), -jnp.inf, v2n)
        front = jnp.where(hit, v1 + v2n, front)
    return count, z


def _peer_route_kernel(a_ref, wq_ref, sk_ref, r2_ref, g2_ref, la_ref, c1_ref):
    qry = jnp.dot(a_ref[...], wq_ref[...], preferred_element_type=F32).astype(BF16)
    for h in range(PEER_HEADS):
        halves = []
        for p in range(2):
            hp = h * 2 + p
            s_t = _dot_nt(sk_ref[hp], qry[:, hp * PEER_KEY_DIM:(hp + 1) * PEER_KEY_DIM])
            vals, rank = _top16_rows(s_t)
            halves.append((s_t, vals, rank))
        (s1, v1, rank1), (s2, v2, rank2) = halves
        count, z = _pair_merge(v1, v2)
        la = jnp.zeros_like(rank1)
        for i in range(PEER_TOPK):
            la = la + jnp.where(rank1 == float(i), count[i:i + 1, :], 0.0)
        r2_ref[h] = rank2.astype(r2_ref.dtype)
        g2_ref[h] = (jnp.exp(s2 - v2[0:1, :]) / z).astype(g2_ref.dtype)
        la_ref[h] = la
        c1_ref[h] = jnp.exp(s1 - v1[0:1, :])


def _peer_route(a3, w_pq, sub_keys, tn):
    n, d = a3.shape
    nk = sub_keys.shape[1]
    tab = lambda dt: jax.ShapeDtypeStruct((PEER_HEADS, nk, n), dt)
    tspec = pl.BlockSpec((PEER_HEADS, nk, tn), lambda i: (0, 0, i))
    return pl.pallas_call(
        _peer_route_kernel,
        grid=(n // tn,),
        in_specs=[pl.BlockSpec((tn, d), lambda i: (i, 0)), _const_spec(w_pq.shape), _const_spec(sub_keys.shape)],
        out_specs=[tspec] * 4,
        out_shape=[tab(BF16), tab(BF16), tab(F32), tab(F32)],
        compiler_params=_cparams("parallel"),
        name="peer_route",
    )(a3, w_pq, sub_keys)


PEER_SUB = 2
PEER_GROUPS = 2


def _gelu_exact(x):
    return 0.5 * x * (1.0 + lax.erf(x * (2.0 ** -0.5)))


def _peer_dense_kernel(at_ref, u_ref, vt_ref, r2_ref, g2_ref, la_ref, c1_ref, h_ref, gfin_ref,
                       o_ref, yt_ref, z_ref, p_ref, *, na, final_norm):
    j = pl.program_id(1)

    @pl.when(j == 0)
    def _():
        yt_ref[...] = jnp.zeros(yt_ref.shape, F32)

    nk = PEER_N_KEYS
    tn = at_ref.shape[1]
    nsub = na // PEER_SUB
    rows_of = lambda sub: slice(sub * PEER_SUB * nk, (sub + 1) * PEER_SUB * nk)
    z_of = lambda sub: jnp.dot(u_ref[rows_of(sub), :], at_ref[...], preferred_element_type=F32)
    sub_per_group = nsub // PEER_GROUPS
    group_rows = sub_per_group * PEER_SUB * nk

    def contract(g):
        cols = slice(g * group_rows, (g + 1) * group_rows)
        yt_ref[...] += jnp.dot(vt_ref[:, cols], p_ref[g], preferred_element_type=F32)

    z_ref[0] = z_of(0)
    for sub in range(nsub):
        g, sub_in_g = divmod(sub, sub_per_group)
        if sub + 1 < nsub:
            z_ref[(sub + 1) % 2] = z_of(sub + 1)
        if sub_in_g == 0 and g > 0:
            contract(g - 1)
        z_t = z_ref[sub % 2]
        for ai in range(PEER_SUB):
            a_idx = j * na + sub * PEER_SUB + ai
            w = jnp.zeros((nk, tn), BF16)
            for h in range(PEER_HEADS):
                la = la_ref[h, pl.ds(a_idx, 1), :].astype(BF16)
                c1 = c1_ref[h, pl.ds(a_idx, 1), :].astype(BF16)
                w = w + jnp.where(r2_ref[h] < la, g2_ref[h], jnp.zeros((), BF16)) * c1
            r0 = (sub_in_g * PEER_SUB + ai) * nk
            p_ref[g, r0:r0 + nk, :] = w * _gelu_exact(z_t[ai * nk:(ai + 1) * nk, :]).astype(BF16)
    contract(PEER_GROUPS - 1)

    @pl.when(j == pl.num_programs(1) - 1)
    def _():
        h = h_ref[...] + jnp.transpose(yt_ref[...])
        o_ref[...] = _rmsnorm(h, gfin_ref[...]) if final_norm else h


def _peer_dense(a3_t, u_bf, v_t, tabs, h2, g_final, tn, na, final_norm):
    d, n = a3_t.shape
    n_exp = u_bf.shape[0]
    nk = PEER_N_KEYS
    tspec = pl.BlockSpec((PEER_HEADS, nk, tn), lambda i, j: (0, 0, i))
    return pl.pallas_call(
        functools.partial(_peer_dense_kernel, na=na, final_norm=final_norm),
        grid=(n // tn, n_exp // (na * nk)),
        in_specs=[pl.BlockSpec((d, tn), lambda i, j: (0, i)),
                  pl.BlockSpec((na * nk, d), lambda i, j: (j, 0)),
                  pl.BlockSpec((d, na * nk), lambda i, j: (0, j)),
                  tspec, tspec, tspec, tspec,
                  pl.BlockSpec((tn, d), lambda i, j: (i, 0)),
                  _const_spec((1, d))],
        out_specs=pl.BlockSpec((tn, d), lambda i, j: (i, 0)),
        out_shape=jax.ShapeDtypeStruct((n, d), F32),
        scratch_shapes=[pltpu.VMEM((d, tn), F32), pltpu.VMEM((2, PEER_SUB * nk, tn), F32),
                        pltpu.VMEM((PEER_GROUPS, na * nk // PEER_GROUPS, tn), BF16)],
        compiler_params=_cparams("parallel", "arbitrary"),
        name="peer_dense",
    )(a3_t, u_bf, v_t, *tabs, h2, g_final)


def _split_w_in(w_in, d_model, d_inner, conv_ch, ssd_heads):
    sizes = (ATTN_HEADS * ATTN_HEAD_DIM, ATTN_KV_HEADS * ATTN_HEAD_DIM, ATTN_KV_HEADS * ATTN_HEAD_DIM,
             IDX_HEADS * IDX_HEAD_DIM, IDX_HEAD_DIM, IDX_HEADS, d_inner, conv_ch, ssd_heads, 2 * d_model)
    assert sum(sizes) == w_in.shape[1]
    attn_w = sum(sizes[:6])
    o = attn_w
    w_z = w_in[:, o:o + d_inner]; o += d_inner
    w_xbc = w_in[:, o:o + conv_ch]; o += conv_ch
    w_dt = w_in[:, o:o + ssd_heads]; o += ssd_heads
    w_gate = w_in[:, o:]
    return w_in[:, :attn_w], w_z, w_xbc, w_dt, w_gate


PROJ_TM = 512
MERGE_TM = 256
ROUTE_TN = 256
DENSE_TN = 512
DENSE_NA = 16


def kernel(x, mem, positions, norm_mix_g, w_in, conv_w, conv_b, dt_bias, a_log, d_skip, ssd_norm_g, w_attn_branch, w_ssd_branch, w_out, norm_cross_g, norm_mem_g, w_cross_q, w_cross_kv, w_cross_out, norm_ffn_g, w_peer_q, peer_sub_keys, peer_u, peer_v, norm_final_g):
    batch, seq, d = x.shape
    n = batch * seq
    depth = w_in.shape[0]
    d_inner = ssd_norm_g.shape[1]
    conv_ch = conv_b.shape[1]
    ssd_heads = dt_bias.shape[1]
    row = lambda v: v.reshape(1, -1)
    bf = lambda w: w.astype(BF16)

    h = x.reshape(n, d)
    pos = positions.reshape(n, 1).astype(F32)
    mem2 = mem.reshape(batch * mem.shape[1], d)
    for layer in range(depth):
        w_attn, w_z, w_xbc, w_dt, w_gate = _split_w_in(w_in[layer], d, d_inner, conv_ch, ssd_heads)
        g_mix = row(norm_mix_g[layer])
        qt, k, vt, iqt, ik, iwt = _attn_proj(h, g_mix, w_attn, pos)
        z, xbc, dt = _ssd_proj(h, g_mix, w_z, w_xbc, w_dt, PROJ_TM)
        gates = _rms_proj(_gate_proj_kernel, h, g_mix, bf(w_gate), F32, PROJ_TM, "gate_proj")
        attn = _dsa(qt, k, vt, iqt, ik, iwt, batch, seq)
        ssd = _ssd(z, xbc, dt, conv_w[layer], conv_b[layer], dt_bias[layer], a_log[layer], d_skip[layer],
                   ssd_norm_g[layer], batch, seq)
        kv = _rms_proj(_plain_proj_kernel, mem2, row(norm_mem_g[layer]), bf(w_cross_kv[layer]),
                       BF16, mem.shape[1], "mem_kv")
        h2, a3, a3_t = _merge(h, attn, ssd, gates, bf(w_attn_branch[layer]), bf(w_ssd_branch[layer]),
                              bf(w_out[layer]), row(norm_cross_g[layer]), bf(w_cross_q[layer]), kv,
                              bf(w_cross_out[layer]), row(norm_ffn_g[layer]), batch, seq, MERGE_TM)
        sk = bf(peer_sub_keys[layer].reshape(PEER_HEADS * 2, PEER_N_KEYS, PEER_KEY_DIM))
        tabs = _peer_route(a3, bf(w_peer_q[layer]), sk, ROUTE_TN)
        h = _peer_dense(a3_t, bf(peer_u[layer]), bf(peer_v[layer].T), tabs, h2,
                        row(norm_final_g), DENSE_TN, DENSE_NA, layer == depth - 1)
    return h.reshape(batch, seq, d)
```

```python
import functools
import math

import jax
import jax.numpy as jnp
from jax import lax
from jax.experimental import pallas as pl
from jax.experimental.pallas import tpu as pltpu

F32 = jnp.float32
BF16 = jnp.bfloat16
I32 = jnp.int32

ATTN_HEADS = 8
ATTN_KV_HEADS = 2
ATTN_HEAD_DIM = 128
IDX_HEADS = 4
IDX_HEAD_DIM = 64
TOPK_MAX = 256
ROPE_THETA = 500000.0
ROPE_FRACTION = 4
SSD_HEAD_DIM = 64
SSD_GROUPS = 4
SSD_STATE = 128
SSD_CONV = 4
SSD_CHUNK = 128
MEM_HEADS = 4
PEER_HEADS = 8
PEER_N_KEYS = 128
PEER_KEY_DIM = 128
PEER_TOPK = 16
EPS = 1e-6

LANES = 128
SUBLANES = 8
VMEM_LIMIT = 56 * 1024 * 1024

NEG_BIG = -1e30
INT_MIN = -2 ** 31


def _cparams(*sem):
    return pltpu.CompilerParams(dimension_semantics=sem, vmem_limit_bytes=VMEM_LIMIT)


def _const_spec(shape):
    nd = len(shape)
    return pl.BlockSpec(shape, lambda *_: (0,) * nd, pipeline_mode=pl.Buffered(1))


def _rmsnorm(x, g):
    return x * lax.rsqrt(jnp.mean(x * x, axis=-1, keepdims=True) + EPS) * g


def _split3(a):
    hi = a.astype(BF16)
    r1 = a - hi.astype(F32)
    mid = r1.astype(BF16)
    lo = (r1 - mid.astype(F32)).astype(BF16)
    return hi, mid, lo


def _dot_exact_rhs(a, b_bf16):
    hi, mid, lo = _split3(a)
    d = functools.partial(jnp.dot, preferred_element_type=F32)
    return d(hi, b_bf16) + d(mid, b_bf16) + d(lo, b_bf16)


def _dot_exact_lhs(a_bf16, b):
    hi, mid, lo = _split3(b)
    d = functools.partial(jnp.dot, preferred_element_type=F32)
    return d(a_bf16, hi) + d(a_bf16, mid) + d(a_bf16, lo)


def _dot_nt(a, b):
    return lax.dot_general(a, b, (((1,), (1,)), ((), ())), preferred_element_type=F32)


def _rope(x, cos, sin_signed, first_half, half):
    w = x.shape[-1]
    x_sw = jnp.where(first_half, pltpu.roll(x, w - half, 1), pltpu.roll(x, half, 1))
    return x * cos + x_sw * sin_signed


def _attn_proj_kernel(x_ref, g_ref, w_ref, pos_ref, rope_ref,
                      qt_ref, k_ref, vt_ref, iqt_ref, ik_ref, iwt_ref):
    a = _rmsnorm(x_ref[...], g_ref[...]).astype(BF16)
    p = jnp.dot(a, w_ref[...], preferred_element_type=F32)
    pos = pos_ref[...]
    tabs = []
    for r, half in ((0, ATTN_HEAD_DIM // ROPE_FRACTION // 2), (2, IDX_HEAD_DIM // ROPE_FRACTION // 2)):
        ang = pos * rope_ref[r:r + 1, :]
        sgn = rope_ref[r + 1:r + 2, :]
        tabs.append((jnp.cos(ang), jnp.sin(ang) * sgn, sgn < 0.0, half))
    t128, t64 = tabs
    off = 0
    for h in range(ATTN_HEADS):
        qt_ref[h * LANES:(h + 1) * LANES, :] = jnp.transpose(_rope(p[:, off:off + LANES], *t128)).astype(BF16)
        off += LANES
    for h in range(ATTN_KV_HEADS):
        k_ref[:, h * LANES:(h + 1) * LANES] = _rope(p[:, off:off + LANES], *t128).astype(BF16)
        off += LANES
    for h in range(ATTN_KV_HEADS):
        vt_ref[0, h * LANES:(h + 1) * LANES, :] = jnp.transpose(p[:, off:off + LANES]).astype(BF16)
        off += LANES
    for h in range(IDX_HEADS):
        iqt_ref[h * LANES:(h + 1) * LANES, :] = jnp.transpose(_rope(p[:, off:off + LANES], *t64)).astype(BF16)
        off += LANES
    ik_ref[...] = _rope(p[:, off:off + LANES], *t64).astype(BF16)
    off += LANES
    iwt_ref[...] = jnp.transpose(p[:, off:off + LANES])[0:SUBLANES, :]


def _rope_rows():
    rows = []
    for head_dim in (ATTN_HEAD_DIM, IDX_HEAD_DIM):
        rot = head_dim // ROPE_FRACTION
        half = rot // 2
        inv = ROPE_THETA ** (-2.0 * jnp.arange(half, dtype=F32) / rot)
        inv_row = jnp.zeros((LANES,), F32).at[:rot].set(jnp.concatenate([inv, inv]))
        sgn_row = jnp.zeros((LANES,), F32).at[:half].set(-1.0).at[half:rot].set(1.0)
        rows += [inv_row, sgn_row]
    return jnp.stack(rows)


def _pad_cols(w, width):
    return jnp.pad(w, ((0, 0), (0, width - w.shape[1])))


def _attn_proj(x2, g, w_in, pos_f32):
    n, d = x2.shape
    tm = DSA_KC
    hq = ATTN_HEADS * ATTN_HEAD_DIM
    hkv = ATTN_KV_HEADS * ATTN_HEAD_DIM
    o = 0
    wq = w_in[:, o:o + hq]; o += hq
    wk = w_in[:, o:o + hkv]; o += hkv
    wv = w_in[:, o:o + hkv]; o += hkv
    wiq = w_in[:, o:o + IDX_HEADS * IDX_HEAD_DIM]; o += IDX_HEADS * IDX_HEAD_DIM
    wik = w_in[:, o:o + IDX_HEAD_DIM]; o += IDX_HEAD_DIM
    wiw = w_in[:, o:o + IDX_HEADS]; o += IDX_HEADS
    wiq = jnp.pad(wiq.reshape(d, IDX_HEADS, IDX_HEAD_DIM),
                  ((0, 0), (0, 0), (0, LANES - IDX_HEAD_DIM))).reshape(d, IDX_HEADS * LANES)
    w = jnp.concatenate([wq, wk, wv, wiq, _pad_cols(wik, LANES), _pad_cols(wiw, LANES)], axis=1).astype(BF16)
    nw = w.shape[1]
    row = lambda width: pl.BlockSpec((tm, width), lambda i: (i, 0))
    col = lambda height: pl.BlockSpec((height, tm), lambda i: (0, i))
    outs = pl.pallas_call(
        _attn_proj_kernel,
        grid=(n // tm,),
        in_specs=[row(d), _const_spec((1, d)), _const_spec((d, nw)), row(1), _const_spec((4, LANES))],
        out_specs=[col(hq), row(hkv), pl.BlockSpec((1, hkv, tm), lambda i: (i, 0, 0)),
                   col(IDX_HEADS * LANES), row(LANES), col(SUBLANES)],
        out_shape=[jax.ShapeDtypeStruct((hq, n), BF16), jax.ShapeDtypeStruct((n, hkv), BF16),
                   jax.ShapeDtypeStruct((n // tm, hkv, tm), BF16),
                   jax.ShapeDtypeStruct((IDX_HEADS * LANES, n), BF16),
                   jax.ShapeDtypeStruct((n, LANES), BF16), jax.ShapeDtypeStruct((SUBLANES, n), F32)],
        compiler_params=_cparams("parallel"),
        name="attn_proj",
    )(x2, g, w, pos_f32, _rope_rows())
    return outs


def _ssd_proj_kernel(x_ref, g_ref, w_ref, z_ref, xbc_ref, dt_ref, *, dz, dxbc):
    a = _rmsnorm(x_ref[...], g_ref[...]).astype(BF16)
    z_ref[...] = jnp.dot(a, w_ref[:, :dz], preferred_element_type=F32)
    xbc_ref[...] = jnp.dot(a, w_ref[:, dz:dz + dxbc], preferred_element_type=F32)
    dt_ref[...] = jnp.dot(a, w_ref[:, dz + dxbc:], preferred_element_type=F32)


def _gate_proj_kernel(x_ref, g_ref, w_ref, o_ref):
    a = _rmsnorm(x_ref[...], g_ref[...]).astype(BF16)
    o_ref[...] = jax.nn.sigmoid(jnp.dot(a, w_ref[...], preferred_element_type=F32))


def _plain_proj_kernel(x_ref, g_ref, w_ref, o_ref):
    a = _rmsnorm(x_ref[...], g_ref[...]).astype(BF16)
    o_ref[...] = jnp.dot(a, w_ref[...], preferred_element_type=F32).astype(o_ref.dtype)


def _rms_proj(body, x2, g, w, out_dtype, tm, name):
    n, d = x2.shape
    nw = w.shape[1]
    return pl.pallas_call(
        body,
        grid=(n // tm,),
        in_specs=[pl.BlockSpec((tm, d), lambda i: (i, 0)), _const_spec((1, d)), _const_spec((d, nw))],
        out_specs=pl.BlockSpec((tm, nw), lambda i: (i, 0)),
        out_shape=jax.ShapeDtypeStruct((n, nw), out_dtype),
        compiler_params=_cparams("parallel"),
        name=name,
    )(x2, g, w)


def _ssd_proj(x2, g, w_z, w_xbc, w_dt, tm):
    n, d = x2.shape
    dz, dxbc = w_z.shape[1], w_xbc.shape[1]
    w = jnp.concatenate([w_z, w_xbc, _pad_cols(w_dt, LANES)], axis=1).astype(BF16)
    row = lambda width: pl.BlockSpec((tm, width), lambda i: (i, 0))
    return pl.pallas_call(
        functools.partial(_ssd_proj_kernel, dz=dz, dxbc=dxbc),
        grid=(n // tm,),
        in_specs=[row(d), _const_spec((1, d)), _const_spec((d, w.shape[1]))],
        out_specs=[row(dz), row(dxbc), row(LANES)],
        out_shape=[jax.ShapeDtypeStruct((n, dz), F32), jax.ShapeDtypeStruct((n, dxbc), F32),
                   jax.ShapeDtypeStruct((n, LANES), F32)],
        compiler_params=_cparams("parallel"),
        name="ssd_proj",
    )(x2, g, w)


DSA_QB = 128
DSA_KC = 512
COUNT_CHAINS = 8


def _dsa_kernel(iqt_ref, iwt_ref, qt_ref, ik_ref, k_ref, vt_ref, tri_ref, o_ref,
                key_ref, m_ref, l_ref, acc_ref, s_ref, *, nsel):
    qb = pl.program_id(1)
    nchunk = qb // (DSA_KC // DSA_QB) + 1
    grp = ATTN_HEADS // ATTN_KV_HEADS
    idx_scale = IDX_HEAD_DIM ** -0.5
    w_scale = IDX_HEADS ** -0.5
    exp2_scale = ATTN_HEAD_DIM ** -0.5 * math.log2(math.e)

    t_glob = qb * DSA_QB + lax.broadcasted_iota(I32, (DSA_KC, DSA_QB), 1)
    s_loc = lax.broadcasted_iota(I32, (DSA_KC, DSA_QB), 0)
    t_glob_b = qb * DSA_QB + lax.broadcasted_iota(I32, (DSA_QB, DSA_QB), 1)
    s_loc_b = lax.broadcasted_iota(I32, (DSA_QB, DSA_QB), 0)
    iw_t = iwt_ref[...] * w_scale
    iq_all = jnp.concatenate([iqt_ref[h * LANES:(h + 1) * LANES, :] for h in range(IDX_HEADS)], axis=1)
    q_all = [jnp.concatenate([qt_ref[(kvh * grp + g) * LANES:(kvh * grp + g + 1) * LANES, :]
                              for g in range(grp)], axis=1) for kvh in range(ATTN_KV_HEADS)]

    def score_chunk(c, carry):
        ikc = ik_ref[pl.ds(c * DSA_KC, DSA_KC), :]
        d = jnp.dot(ikc, iq_all, preferred_element_type=F32)
        sc = jnp.zeros((DSA_KC, DSA_QB), F32)
        for h in range(IDX_HEADS):
            sc = sc + jnp.maximum(d[:, h * DSA_QB:(h + 1) * DSA_QB] * idx_scale, 0.0) * iw_t[h:h + 1, :]
        sc = jnp.where(c * DSA_KC + s_loc <= t_glob, sc, -jnp.inf)
        key_ref[pl.ds(c * DSA_KC, DSA_KC), :] = sc
        return carry

    lax.fori_loop(0, nchunk, score_chunk, 0)

    def key_to_score(k):
        bits = jnp.where(k < 0, k ^ 0x7FFFFFFF, k)
        return jnp.where(k < -2139095041, -jnp.inf, pltpu.bitcast(bits, F32))

    def count_ge(cand):
        cand = key_to_score(cand)
        def body(c, accs):
            accs = list(accs)
            for j in range(DSA_KC // SUBLANES):
                blk = key_ref[pl.ds(c * DSA_KC + j * SUBLANES, SUBLANES), :]
                accs[j % COUNT_CHAINS] = accs[j % COUNT_CHAINS] + jnp.where(blk >= cand, 1, 0)
            return tuple(accs)
        zero = jnp.zeros((SUBLANES, DSA_QB), I32)
        accs = lax.fori_loop(0, nchunk, body, (zero,) * COUNT_CHAINS)
        return jnp.sum(functools.reduce(lambda x, y: x + y, accs), axis=0, keepdims=True)

    def bisect(it, ans):
        cand = ans + lax.shift_left(jnp.int32(1), 31 - it)
        cnt = count_ge(cand)
        return jnp.where(cnt >= nsel, cand, ans)

    tau = lax.fori_loop(0, 32, bisect, jnp.full((SUBLANES, DSA_QB), INT_MIN, I32))
    n_gt = count_ge(tau + 1)
    n_tie = (nsel - n_gt).astype(F32)
    tau_row = key_to_score(tau)[0:1, :]

    m_ref[...] = jnp.full(m_ref.shape, NEG_BIG, F32)
    l_ref[...] = jnp.zeros(l_ref.shape, F32)
    acc_ref[...] = jnp.zeros(acc_ref.shape, F32)

    last = nchunk - 1

    def masked_logits(c, slot, tie_carry):
        cl = jnp.minimum(c, last)
        biases = []
        for sb in range(DSA_KC // DSA_QB):
            keys = key_ref[pl.ds(cl * DSA_KC + sb * DSA_QB, DSA_QB), :]
            eq = keys == tau_row
            prefix = jnp.dot(tri_ref[...], jnp.where(eq, 1.0, 0.0).astype(BF16),
                             preferred_element_type=F32) + tie_carry
            tie_carry = prefix[DSA_QB - 1:DSA_QB, :]
            sel = (keys > tau_row) | (eq & (prefix <= n_tie))
            sel = sel & (c * DSA_KC + sb * DSA_QB + s_loc_b <= t_glob_b)
            biases.append(jnp.where(sel, 0.0, NEG_BIG))
        bias = jnp.concatenate(biases, axis=0)
        bias = jnp.concatenate([bias] * grp, axis=1)
        for kvh in range(ATTN_KV_HEADS):
            kc = k_ref[pl.ds(cl * DSA_KC, DSA_KC), kvh * LANES:(kvh + 1) * LANES]
            s_ref[slot, kvh] = jnp.dot(kc, q_all[kvh], preferred_element_type=F32) + bias
        return tie_carry

    def softmax_pv(c, slot):
        cl = jnp.minimum(c, last)
        for kvh in range(ATTN_KV_HEADS):
            vtc = vt_ref[cl, kvh * LANES:(kvh + 1) * LANES, :]
            s = s_ref[slot, kvh]
            m_old = m_ref[kvh]
            m_new = jnp.maximum(m_old, jnp.max(s, axis=0, keepdims=True))
            p = jnp.exp2((s - m_new) * exp2_scale)
            alpha = jnp.exp2((m_old - m_new) * exp2_scale)
            l_ref[kvh] = alpha * l_ref[kvh] + jnp.sum(p, axis=0, keepdims=True)
            acc_ref[kvh] = alpha * acc_ref[kvh] + jnp.dot(vtc, p.astype(BF16), preferred_element_type=F32)
            m_ref[kvh] = m_new

    def attn_pair(k2, tie_carry):
        c = 2 * k2
        tie_carry = masked_logits(c + 1, 1, tie_carry)
        softmax_pv(c, 0)
        tie_carry = masked_logits(c + 2, 0, tie_carry)
        softmax_pv(c + 1, 1)
        return tie_carry

    tie0 = masked_logits(0, 0, jnp.zeros((1, DSA_QB), F32))
    lax.fori_loop(0, (nchunk + 1) // 2, attn_pair, tie0)
    for kvh in range(ATTN_KV_HEADS):
        o_t = acc_ref[kvh] / l_ref[kvh]
        for g in range(grp):
            h = kvh * grp + g
            o_ref[:, h * LANES:(h + 1) * LANES] = jnp.transpose(
                o_t[:, g * DSA_QB:(g + 1) * DSA_QB]).astype(o_ref.dtype)


def _dsa(qt, k, vt, iqt, ik, iwt, batch, seq):
    n = batch * seq
    nsel = min(TOPK_MAX, seq // 4)
    nqb = seq // DSA_QB
    grp_lanes = ATTN_HEADS // ATTN_KV_HEADS * DSA_QB
    hq = ATTN_HEADS * ATTN_HEAD_DIM
    hkv = ATTN_KV_HEADS * ATTN_HEAD_DIM
    tri = (jnp.arange(DSA_QB)[:, None] >= jnp.arange(DSA_QB)[None, :]).astype(BF16)
    qcol = lambda height: pl.BlockSpec((height, DSA_QB), lambda b, i: (0, b * nqb + i))
    full = lambda width: pl.BlockSpec((seq, width), lambda b, i: (b, 0))
    return pl.pallas_call(
        functools.partial(_dsa_kernel, nsel=nsel),
        grid=(batch, nqb),
        in_specs=[qcol(IDX_HEADS * LANES), qcol(SUBLANES), qcol(hq), full(LANES), full(hkv),
                  pl.BlockSpec((seq // DSA_KC, hkv, DSA_KC), lambda b, i: (b, 0, 0)),
                  _const_spec((DSA_QB, DSA_QB))],
        out_specs=pl.BlockSpec((DSA_QB, hq), lambda b, i: (b * nqb + i, 0)),
        out_shape=jax.ShapeDtypeStruct((n, hq), BF16),
        scratch_shapes=[pltpu.VMEM((seq, DSA_QB), F32),
                        pltpu.VMEM((ATTN_KV_HEADS, 1, grp_lanes), F32),
                        pltpu.VMEM((ATTN_KV_HEADS, 1, grp_lanes), F32),
                        pltpu.VMEM((ATTN_KV_HEADS, ATTN_HEAD_DIM, grp_lanes), F32),
                        pltpu.VMEM((2, ATTN_KV_HEADS, DSA_KC, grp_lanes), F32)],
        compiler_params=_cparams("arbitrary", "arbitrary"),
        name="dsa",
    )(iqt, iwt, qt, ik, k, vt, tri)


CONV_HALO = 8


def _ssd_kernel(xbc_ref, z_ref, dt_ref, cw_ref, cb_ref, dtb_ref, alog_ref, dskip_ref, ng_ref,
                tri_ref, exp_ref, o_ref, ext_ref, state_ref, *, d_inner):
    L = SSD_CHUNK
    gn = SSD_GROUPS * SSD_STATE
    heads_per_group = d_inner // SSD_HEAD_DIM // SSD_GROUPS
    gw = heads_per_group * SSD_HEAD_DIM

    @pl.when(pl.program_id(1) == 0)
    def _():
        ext_ref[0:CONV_HALO, :] = jnp.zeros((CONV_HALO, ext_ref.shape[1]), F32)
        state_ref[...] = jnp.zeros(state_ref.shape, F32)

    raw = xbc_ref[...]
    ext_ref[CONV_HALO:CONV_HALO + L, :] = raw
    acc = cb_ref[...] + jnp.zeros_like(raw)
    for kk in range(SSD_CONV):
        start = CONV_HALO - (SSD_CONV - 1) + kk
        acc = acc + cw_ref[kk:kk + 1, :] * ext_ref[start:start + L, :]
    ext_ref[0:CONV_HALO, :] = raw[L - CONV_HALO:L, :]
    xbc = acc * jax.nn.sigmoid(acc)
    xs = xbc[:, :d_inner]
    bm = xbc[:, d_inner:d_inner + gn]
    cm = xbc[:, d_inner + gn:]

    dt = jax.nn.softplus(dt_ref[...] + dtb_ref[...])
    a = -jnp.exp(alog_ref[...])
    acs = _dot_exact_lhs(tri_ref[...], dt * a)
    acs_t = jnp.transpose(acs)
    e_acs = jnp.exp(acs)
    e_end = jnp.exp(acs[L - 1:L, :] - acs)
    expand = exp_ref[...]
    dt_x = _dot_exact_rhs(dt, expand)
    e_acs_x = _dot_exact_rhs(e_acs, expand)
    e_end_x = _dot_exact_rhs(e_end, expand)
    xdt = xs * dt_x
    xdt_b = xdt.astype(BF16)
    xw_b = (xdt * e_end_x).astype(BF16)

    lower = lax.broadcasted_iota(I32, (L, L), 0) >= lax.broadcasted_iota(I32, (L, L), 1)
    lane = lax.broadcasted_iota(I32, (L, LANES), 1)
    heads_per_tile = LANES // SSD_HEAD_DIM
    for g in range(SSD_GROUPS):
        cg = cm[:, g * SSD_STATE:(g + 1) * SSD_STATE].astype(BF16)
        bg = bm[:, g * SSD_STATE:(g + 1) * SSD_STATE]
        cb = _dot_nt(cg, bg.astype(BF16))
        st = state_ref[:, g * gw:(g + 1) * gw]
        y_off = jnp.dot(cg, st.astype(BF16), preferred_element_type=F32) * e_acs_x[:, g * gw:(g + 1) * gw]
        for tile in range(gw // LANES):
            col = g * gw + tile * LANES
            x_tile = xdt_b[:, col:col + LANES]
            y_tile = y_off[:, tile * LANES:(tile + 1) * LANES]
            for sub in range(heads_per_tile):
                h = col // SSD_HEAD_DIM + sub
                seg = acs[:, h:h + 1] - acs_t[h:h + 1, :]
                m = (cb * jnp.exp(jnp.where(lower, seg, -jnp.inf))).astype(BF16)
                in_head = (lane >= sub * SSD_HEAD_DIM) & (lane < (sub + 1) * SSD_HEAD_DIM)
                y_tile = y_tile + jnp.dot(m, jnp.where(in_head, x_tile, jnp.zeros_like(x_tile)),
                                          preferred_element_type=F32)
            y_tile = y_tile + xs[:, col:col + LANES] * dskip_ref[:, col:col + LANES]
            zt = z_ref[:, col:col + LANES]
            o_ref[:, col:col + LANES] = (y_tile * (zt * jax.nn.sigmoid(zt))).astype(o_ref.dtype)
        bg_t = jnp.transpose(bg).astype(BF16)
        state_ref[:, g * gw:(g + 1) * gw] = (
            st * e_acs_x[L - 1:L, g * gw:(g + 1) * gw]
            + jnp.dot(bg_t, xw_b[:, g * gw:(g + 1) * gw], preferred_element_type=F32))

    y = o_ref[...].astype(F32)
    o_ref[...] = _rmsnorm(y, ng_ref[...]).astype(o_ref.dtype)


def _ssd(z, xbc, dt, conv_w, conv_b, dt_bias, a_log, d_skip, norm_g, batch, seq):
    n, d_inner = z.shape
    cch = xbc.shape[1]
    heads = d_inner // SSD_HEAD_DIM
    nc = seq // SSD_CHUNK
    L = SSD_CHUNK
    tri = (jnp.arange(L)[:, None] >= jnp.arange(L)[None, :]).astype(BF16)
    expand = (jnp.arange(LANES)[:, None] == (jnp.arange(d_inner)[None, :] // SSD_HEAD_DIM)).astype(BF16)
    pad_h = lambda v: jnp.pad(v.reshape(1, heads), ((0, 0), (0, LANES - heads)))
    row = lambda width: pl.BlockSpec((L, width), lambda b, c: (b * nc + c, 0))
    return pl.pallas_call(
        functools.partial(_ssd_kernel, d_inner=d_inner),
        grid=(batch, nc),
        in_specs=[row(cch), row(d_inner), row(LANES),
                  _const_spec((SSD_CONV, cch)), _const_spec((1, cch)), _const_spec((1, LANES)),
                  _const_spec((1, LANES)), _const_spec((1, d_inner)), _const_spec((1, d_inner)),
                  _const_spec((L, L)), _const_spec((LANES, d_inner))],
        out_specs=row(d_inner),
        out_shape=jax.ShapeDtypeStruct((n, d_inner), F32),
        scratch_shapes=[pltpu.VMEM((CONV_HALO + L, cch), F32), pltpu.VMEM((SSD_STATE, d_inner), F32)],
        compiler_params=_cparams("arbitrary", "arbitrary"),
        name="ssd",
    )(xbc, z, dt, conv_w.reshape(SSD_CONV, cch), conv_b.reshape(1, cch), pad_h(dt_bias), pad_h(a_log),
      jnp.repeat(d_skip, SSD_HEAD_DIM).reshape(1, d_inner), norm_g.reshape(1, d_inner), tri, expand)


def _merge_kernel(x_ref, attn_ref, ssd_ref, gate_ref, wab_ref, wsb_ref, wout_ref, gc_ref, wcq_ref,
                  kv_ref, wco_ref, gf_ref, h_ref, a_ref, at_ref, *, d_model):
    dot = functools.partial(jnp.dot, preferred_element_type=F32)
    br_a = dot(attn_ref[...], wab_ref[...])
    br_s = dot(ssd_ref[...].astype(BF16), wsb_ref[...])
    merged = gate_ref[:, :d_model] * br_a + gate_ref[:, d_model:] * br_s
    h = x_ref[...] + dot(merged.astype(BF16), wout_ref[...])

    qc = dot(_rmsnorm(h, gc_ref[...]).astype(BF16), wcq_ref[...]).astype(BF16)
    dh = d_model // MEM_HEADS
    outs = []
    for hd in range(MEM_HEADS):
        kk = kv_ref[:, hd * dh:(hd + 1) * dh]
        vv = kv_ref[:, d_model + hd * dh:d_model + (hd + 1) * dh]
        logits = _dot_nt(qc[:, hd * dh:(hd + 1) * dh], kk) * dh ** -0.5
        e = jnp.exp(logits - jnp.max(logits, axis=-1, keepdims=True))
        p = e / jnp.sum(e, axis=-1, keepdims=True)
        outs.append(dot(p.astype(BF16), vv))
    o = jnp.concatenate(outs, axis=-1)
    h = h + dot(o.astype(BF16), wco_ref[...])
    h_ref[...] = h
    a = _rmsnorm(h, gf_ref[...])
    a_ref[...] = a.astype(a_ref.dtype)
    at_ref[...] = jnp.transpose(a).astype(at_ref.dtype)


def _merge(x2, attn, ssd, gates, w_ab, w_sb, w_out, g_cross, w_cq, kv, w_co, g_ffn, batch, seq, tm):
    n, d = x2.shape
    mem_len = kv.shape[0] // batch
    per_b = seq // tm
    row = lambda width: pl.BlockSpec((tm, width), lambda i: (i, 0))
    return pl.pallas_call(
        functools.partial(_merge_kernel, d_model=d),
        grid=(n // tm,),
        in_specs=[row(d), row(attn.shape[1]), row(ssd.shape[1]), row(gates.shape[1]),
                  _const_spec(w_ab.shape), _const_spec(w_sb.shape), _const_spec(w_out.shape),
                  _const_spec((1, d)), _const_spec(w_cq.shape),
                  pl.BlockSpec((mem_len, kv.shape[1]), lambda i: (i // per_b, 0)),
                  _const_spec(w_co.shape), _const_spec((1, d))],
        out_specs=[row(d), row(d), pl.BlockSpec((d, tm), lambda i: (0, i))],
        out_shape=[jax.ShapeDtypeStruct((n, d), F32), jax.ShapeDtypeStruct((n, d), BF16),
                   jax.ShapeDtypeStruct((d, n), BF16)],
        compiler_params=_cparams("parallel"),
        name="merge",
    )(x2, attn, ssd, gates, w_ab, w_sb, w_out, g_cross, w_cq, kv, w_co, g_ffn)


def _top16_extract(s_t, stable):
    nk, tn = s_t.shape
    key_iota = lax.broadcasted_iota(I32, (nk, tn), 0).astype(F32)
    slot_iota = lax.broadcasted_iota(I32, (PEER_TOPK, tn), 0)
    cur = s_t
    rank = jnp.full((nk, tn), float(PEER_TOPK), F32)
    vals = jnp.zeros((PEER_TOPK, tn), F32)
    for r in range(PEER_TOPK):
        m = jnp.max(cur, axis=0, keepdims=True)
        hit = cur == m
        if stable:
            first = jnp.min(jnp.where(hit, key_iota, float(nk)), axis=0, keepdims=True)
            hit = key_iota == first
        rank = jnp.where(hit, float(r), rank)
        cur = jnp.where(hit, -jnp.inf, cur)
        vals = jnp.where(slot_iota == r, m, vals)
    return vals, rank


def _pair_merge(v1, v2):
    k, tn = v1.shape
    slot = lax.broadcasted_iota(I32, (k, tn), 0).astype(F32)
    slot_f = slot
    count = jnp.zeros((k, tn), F32)
    front = v1 + v2[0:1, :]
    best = front[0:1, :]
    z = jnp.zeros((1, tn), F32)
    for _ in range(PEER_TOPK):
        m = jnp.max(front, axis=0, keepdims=True)
        first = jnp.min(jnp.where(front == m, slot, float(k)), axis=0, keepdims=True)
        hit = slot == first
        z = z + jnp.exp(m - best)
        count = jnp.where(hit, count + 1.0, count)
        nxt = jnp.sum(jnp.where(hit, count, 0.0), axis=0, keepdims=True)
        v2n = jnp.sum(jnp.where(slot_f == nxt, v2, 0.0), axis=0, keepdims=True)
        v2n = jnp.where(nxt >= float(k), -jnp.inf, v2n)
        front = jnp.where(hit, v1 + v2n, front)
    return count, z


def _route_tables(qry, sk_ref, r2_ref, g2_ref, la_ref, c1_ref, stable):
    most_taken = jnp.zeros((1, qry.shape[0]), F32)
    for h in range(PEER_HEADS):
        halves = []
        for p in range(2):
            hp = h * 2 + p
            s_t = _dot_nt(sk_ref[hp], qry[:, hp * PEER_KEY_DIM:(hp + 1) * PEER_KEY_DIM])
            vals, rank = _top16_extract(s_t, stable)
            taken = jnp.sum(jnp.where(rank < float(PEER_TOPK), 1.0, 0.0), axis=0, keepdims=True)
            most_taken = jnp.maximum(most_taken, taken)
            halves.append((s_t, vals, rank))
        (s1, v1, rank1), (s2, v2, rank2) = halves
        count, z = _pair_merge(v1, v2)
        la = jnp.zeros_like(rank1)
        for i in range(PEER_TOPK):
            la = la + jnp.where(rank1 == float(i), count[i:i + 1, :], 0.0)
        r2_ref[h] = rank2.astype(r2_ref.dtype)
        g2_ref[h] = (jnp.exp(s2 - v2[0:1, :]) / z).astype(g2_ref.dtype)
        la_ref[h] = la
        c1_ref[h] = jnp.exp(s1 - v1[0:1, :])
    return jnp.max(most_taken)


def _peer_route_kernel(a_ref, wq_ref, sk_ref, r2_ref, g2_ref, la_ref, c1_ref):
    qry = jnp.dot(a_ref[...], wq_ref[...], preferred_element_type=F32).astype(BF16)
    refs = (r2_ref, g2_ref, la_ref, c1_ref)
    most_taken = _route_tables(qry, sk_ref, *refs, stable=False)

    @pl.when(most_taken > float(PEER_TOPK))
    def _():
        _route_tables(qry, sk_ref, *refs, stable=True)


def _peer_route(a3, w_pq, sub_keys, tn):
    n, d = a3.shape
    nk = sub_keys.shape[1]
    tab = lambda dt: jax.ShapeDtypeStruct((PEER_HEADS, nk, n), dt)
    tspec = pl.BlockSpec((PEER_HEADS, nk, tn), lambda i: (0, 0, i))
    return pl.pallas_call(
        _peer_route_kernel,
        grid=(n // tn,),
        in_specs=[pl.BlockSpec((tn, d), lambda i: (i, 0)), _const_spec(w_pq.shape), _const_spec(sub_keys.shape)],
        out_specs=[tspec] * 4,
        out_shape=[tab(BF16), tab(BF16), tab(F32), tab(F32)],
        compiler_params=_cparams("parallel"),
        name="peer_route",
    )(a3, w_pq, sub_keys)


PEER_SUB = 2
PEER_GROUPS = 2


def _gelu_exact(x):
    return 0.5 * x * (1.0 + lax.erf(x * (2.0 ** -0.5)))


def _peer_dense_kernel(at_ref, u_ref, vt_ref, r2_ref, g2_ref, la_ref, c1_ref, h_ref, gfin_ref,
                       o_ref, yt_ref, z_ref, p_ref, *, na, final_norm):
    j = pl.program_id(1)

    @pl.when(j == 0)
    def _():
        yt_ref[...] = jnp.zeros(yt_ref.shape, F32)

    nk = PEER_N_KEYS
    tn = at_ref.shape[1]
    nsub = na // PEER_SUB
    rows_of = lambda sub: slice(sub * PEER_SUB * nk, (sub + 1) * PEER_SUB * nk)
    z_of = lambda sub: jnp.dot(u_ref[rows_of(sub), :], at_ref[...], preferred_element_type=F32)
    sub_per_group = nsub // PEER_GROUPS
    group_rows = sub_per_group * PEER_SUB * nk

    def contract(g):
        cols = slice(g * group_rows, (g + 1) * group_rows)
        yt_ref[...] += jnp.dot(vt_ref[:, cols], p_ref[g], preferred_element_type=F32)

    z_ref[0] = z_of(0)
    for sub in range(nsub):
        g, sub_in_g = divmod(sub, sub_per_group)
        if sub + 1 < nsub:
            z_ref[(sub + 1) % 2] = z_of(sub + 1)
        if sub_in_g == 0 and g > 0:
            contract(g - 1)
        z_t = z_ref[sub % 2]
        for ai in range(PEER_SUB):
            a_idx = j * na + sub * PEER_SUB + ai
            w = jnp.zeros((nk, tn), BF16)
            for h in range(PEER_HEADS):
                la = la_ref[h, pl.ds(a_idx, 1), :].astype(BF16)
                c1 = c1_ref[h, pl.ds(a_idx, 1), :].astype(BF16)
                w = w + jnp.where(r2_ref[h] < la, g2_ref[h], jnp.zeros((), BF16)) * c1
            r0 = (sub_in_g * PEER_SUB + ai) * nk
            p_ref[g, r0:r0 + nk, :] = w * _gelu_exact(z_t[ai * nk:(ai + 1) * nk, :]).astype(BF16)
    contract(PEER_GROUPS - 1)

    @pl.when(j == pl.num_programs(1) - 1)
    def _():
        h = h_ref[...] + jnp.transpose(yt_ref[...])
        o_ref[...] = _rmsnorm(h, gfin_ref[...]) if final_norm else h


def _peer_dense(a3_t, u_bf, v_t, tabs, h2, g_final, tn, na, final_norm):
    d, n = a3_t.shape
    n_exp = u_bf.shape[0]
    nk = PEER_N_KEYS
    tspec = pl.BlockSpec((PEER_HEADS, nk, tn), lambda i, j: (0, 0, i))
    return pl.pallas_call(
        functools.partial(_peer_dense_kernel, na=na, final_norm=final_norm),
        grid=(n // tn, n_exp // (na * nk)),
        in_specs=[pl.BlockSpec((d, tn), lambda i, j: (0, i)),
                  pl.BlockSpec((na * nk, d), lambda i, j: (j, 0)),
                  pl.BlockSpec((d, na * nk), lambda i, j: (0, j)),
                  tspec, tspec, tspec, tspec,
                  pl.BlockSpec((tn, d), lambda i, j: (i, 0)),
                  _const_spec((1, d))],
        out_specs=pl.BlockSpec((tn, d), lambda i, j: (i, 0)),
        out_shape=jax.ShapeDtypeStruct((n, d), F32),
        scratch_shapes=[pltpu.VMEM((d, tn), F32), pltpu.VMEM((2, PEER_SUB * nk, tn), F32),
                        pltpu.VMEM((PEER_GROUPS, na * nk // PEER_GROUPS, tn), BF16)],
        compiler_params=_cparams("parallel", "arbitrary"),
        name="peer_dense",
    )(a3_t, u_bf, v_t, *tabs, h2, g_final)


def _split_w_in(w_in, d_model, d_inner, conv_ch, ssd_heads):
    sizes = (ATTN_HEADS * ATTN_HEAD_DIM, ATTN_KV_HEADS * ATTN_HEAD_DIM, ATTN_KV_HEADS * ATTN_HEAD_DIM,
             IDX_HEADS * IDX_HEAD_DIM, IDX_HEAD_DIM, IDX_HEADS, d_inner, conv_ch, ssd_heads, 2 * d_model)
    assert sum(sizes) == w_in.shape[1]
    attn_w = sum(sizes[:6])
    o = attn_w
    w_z = w_in[:, o:o + d_inner]; o += d_inner
    w_xbc = w_in[:, o:o + conv_ch]; o += conv_ch
    w_dt = w_in[:, o:o + ssd_heads]; o += ssd_heads
    w_gate = w_in[:, o:]
    return w_in[:, :attn_w], w_z, w_xbc, w_dt, w_gate


PROJ_TM = 512
MERGE_TM = 256
ROUTE_TN = 256
DENSE_TN = 512
DENSE_NA = 16


def kernel(x, mem, positions, norm_mix_g, w_in, conv_w, conv_b, dt_bias, a_log, d_skip, ssd_norm_g, w_attn_branch, w_ssd_branch, w_out, norm_cross_g, norm_mem_g, w_cross_q, w_cross_kv, w_cross_out, norm_ffn_g, w_peer_q, peer_sub_keys, peer_u, peer_v, norm_final_g):
    batch, seq, d = x.shape
    n = batch * seq
    depth = w_in.shape[0]
    d_inner = ssd_norm_g.shape[1]
    conv_ch = conv_b.shape[1]
    ssd_heads = dt_bias.shape[1]
    row = lambda v: v.reshape(1, -1)
    bf = lambda w: w.astype(BF16)

    h = x.reshape(n, d)
    pos = positions.reshape(n, 1).astype(F32)
    mem2 = mem.reshape(batch * mem.shape[1], d)
    for layer in range(depth):
        w_attn, w_z, w_xbc, w_dt, w_gate = _split_w_in(w_in[layer], d, d_inner, conv_ch, ssd_heads)
        g_mix = row(norm_mix_g[layer])
        qt, k, vt, iqt, ik, iwt = _attn_proj(h, g_mix, w_attn, pos)
        z, xbc, dt = _ssd_proj(h, g_mix, w_z, w_xbc, w_dt, PROJ_TM)
        gates = _rms_proj(_gate_proj_kernel, h, g_mix, bf(w_gate), F32, PROJ_TM, "gate_proj")
        attn = _dsa(qt, k, vt, iqt, ik, iwt, batch, seq)
        ssd = _ssd(z, xbc, dt, conv_w[layer], conv_b[layer], dt_bias[layer], a_log[layer], d_skip[layer],
                   ssd_norm_g[layer], batch, seq)
        kv = _rms_proj(_plain_proj_kernel, mem2, row(norm_mem_g[layer]), bf(w_cross_kv[layer]),
                       BF16, mem.shape[1], "mem_kv")
        h2, a3, a3_t = _merge(h, attn, ssd, gates, bf(w_attn_branch[layer]), bf(w_ssd_branch[layer]),
                              bf(w_out[layer]), row(norm_cross_g[layer]), bf(w_cross_q[layer]), kv,
                              bf(w_cross_out[layer]), row(norm_ffn_g[layer]), batch, seq, MERGE_TM)
        sk = bf(peer_sub_keys[layer].reshape(PEER_HEADS * 2, PEER_N_KEYS, PEER_KEY_DIM))
        tabs = _peer_route(a3, bf(w_peer_q[layer]), sk, ROUTE_TN)
        h = _peer_dense(a3_t, bf(peer_u[layer]), bf(peer_v[layer].T), tabs, h2,
                        row(norm_final_g), DENSE_TN, DENSE_NA, layer == depth - 1)
    return h.reshape(batch, seq, d)
```
